```python
import math, functools
import jax, jax.numpy as jnp
from jax import lax
import numpy as np

D_MODEL = 2048
BATCH = 2
SEQ = 8192
DEPTH = 4

CTX_LEN = 256
GRID_W = 64
N_BRANCH = 4
BRANCH_W = 1024
N_MOD = 9
D_FF = 5632
EPS = 1e-6

RET_HEADS = 4
RET_DK = 128
RET_DV = 256
RET_CHUNK = 128
ROPE_BASE = 10000.0

S5_GROUP = 16
S5_GROUPS = BRANCH_W // S5_GROUP
S5_STATE = 64
S5_CHUNK = 128

GLA_HEADS = 4
GLA_DK = 128
GLA_DV = 256
GLA_RANK = 16
GLA_TAU = 16.0
GLA_CHUNK = 64

HY_ORDER = 2
HY_SHORT = 3
HY_BANDS = 16
HY_EMB = 1 + 2 * HY_BANDS
HY_HIDDEN = 64
HY_TARGET = 1e-2
HY_FAST = 0.3
HY_SLOW = 1.5
HY_SHIFT = 0.05

IN_SPLITS = (RET_HEADS * RET_DK, RET_HEADS * RET_DK, RET_HEADS * RET_DV, RET_HEADS * RET_DV,
             BRANCH_W,
             GLA_HEADS * GLA_DK, GLA_HEADS * GLA_DK, GLA_HEADS * GLA_DV, GLA_HEADS * GLA_DV, GLA_RANK, GLA_RANK,
             (HY_ORDER + 1) * BRANCH_W)
IN_COLS = sum(IN_SPLITS)

kernel_name = 'hybrid_retention_s5_gla_hyena_block'


def rms_norm(x, gain):
    xf = x.astype(jnp.float32)
    y = xf * lax.rsqrt(jnp.mean(xf * xf, axis=-1, keepdims=True) + EPS)
    return (y * gain.astype(jnp.float32)).astype(x.dtype)


def modulate(h, gain, shift, scale):
    return rms_norm(h, gain) * (1.0 + scale) + shift


def swiglu(x, w_gate, w_up, w_down):
    return (jax.nn.silu(x @ w_gate) * (x @ w_up)) @ w_down


def axial_rope(rows):
    row = jnp.repeat(jnp.arange(rows, dtype=jnp.float32), GRID_W)
    col = jnp.tile(jnp.arange(GRID_W, dtype=jnp.float32), rows)
    n_freq = RET_DK // 4
    inv = ROPE_BASE ** (-jnp.arange(n_freq, dtype=jnp.float32) / n_freq)
    ang = jnp.concatenate([row[:, None] * inv, col[:, None] * inv], axis=-1)
    return jnp.cos(ang), jnp.sin(ang)


def apply_rope(x, cos, sin):
    half = x.shape[-1] // 2
    x1, x2 = x[..., :half], x[..., half:]
    c_, s_ = cos[None, :, None, :], sin[None, :, None, :]
    return jnp.concatenate([x1 * c_ - x2 * s_, x1 * s_ + x2 * c_], axis=-1)


def to_chunks(a, chunk):
    b, l = a.shape[:2]
    return jnp.moveaxis(a.reshape(b, l // chunk, chunk, *a.shape[2:]), 1, 0)


def from_chunks(a):
    a = jnp.moveaxis(a, 0, 1)
    return a.reshape(a.shape[0], a.shape[1] * a.shape[2], *a.shape[3:])


def bidirectional_prefix_scan(scan_f, scan_b, ctx_f, lat_f, ctx_b, lat_b, state0):
    yc_f, sc_f = scan_f(ctx_f, state0)
    yl_f, _ = scan_f(lat_f, sc_f)
    flip = lambda t: tuple(jnp.flip(a, axis=1) for a in t)
    yc_b, sc_b = scan_b(flip(ctx_b), state0)
    yl_b, _ = scan_b(flip(lat_b), sc_b)
    return yc_f + jnp.flip(yc_b, axis=1), yl_f + jnp.flip(yl_b, axis=1)


def retention_chunked(inputs, state0, log_gamma):
    q, k, v = inputs
    C = RET_CHUNK
    idx = jnp.arange(C, dtype=jnp.float32)
    rel = idx[:, None] - idx[None, :]
    decay_intra = jnp.where(rel >= 0, jnp.exp(log_gamma[:, None, None] * jnp.maximum(rel, 0.0)), 0.0)
    decay_q = jnp.exp((idx[:, None] + 1.0) * log_gamma[None, :])
    decay_k = jnp.exp((C - 1.0 - idx[:, None]) * log_gamma[None, :])
    decay_c = jnp.exp(C * log_gamma)

    def step(S, blk):
        qb, kb, vb = blk
        scores = jnp.einsum('bthd,bshd->bhts', qb, kb) * decay_intra
        o = jnp.einsum('bhts,bshv->bthv', scores, vb)
        o = o + jnp.einsum('bthd,bhdv->bthv', qb * decay_q[None, :, :, None], S)
        S = S * decay_c[None, :, None, None] + jnp.einsum('bshd,bshv->bhdv', kb * decay_k[None, :, :, None], vb)
        return S, o

    S, o = lax.scan(step, state0, (to_chunks(q, C), to_chunks(k, C), to_chunks(v, C)))
    return from_chunks(o), S


def retention_branch(pc, pl, rope_cos, rope_sin, decay_logit, norm_gain):
    f32 = jnp.float32
    heads = lambda t, d: t.astype(f32).reshape(t.shape[0], t.shape[1], RET_HEADS, d)
    scale = RET_DK ** -0.5
    qc, kc, vc, gc = pc
    ql, kl, vl, gl = pl
    ctx_in = (heads(qc, RET_DK), heads(kc, RET_DK) * scale, heads(vc, RET_DV))
    lat_in = (apply_rope(heads(ql, RET_DK), rope_cos, rope_sin),
              apply_rope(heads(kl, RET_DK), rope_cos, rope_sin) * scale,
              heads(vl, RET_DV))
    log_gamma = jax.nn.log_sigmoid(decay_logit.astype(f32))
    state0 = jnp.zeros((qc.shape[0], RET_HEADS, RET_DK, RET_DV), f32)
    oc, ol = bidirectional_prefix_scan(functools.partial(retention_chunked, log_gamma=log_gamma[0]),
                                       functools.partial(retention_chunked, log_gamma=log_gamma[1]),
                                       ctx_in, lat_in, ctx_in, lat_in, state0)
    gain = norm_gain.astype(f32).reshape(RET_HEADS, RET_DV)

    def out(o, g):
        mu = jnp.mean(o, axis=-1, keepdims=True)
        var = jnp.mean(jnp.square(o - mu), axis=-1, keepdims=True)
        on = (o - mu) * lax.rsqrt(var + EPS) * gain
        return jax.nn.silu(g.astype(f32)) * on.reshape(o.shape[0], o.shape[1], -1)

    return out(oc, gc), out(ol, gl)


def s5_discretise(a_re, a_im, log_dt, b_re, b_im):
    dt = jnp.exp(log_dt)[:, None]
    mag = jnp.exp(a_re * dt)
    ab_re, ab_im = mag * jnp.cos(a_im * dt), mag * jnp.sin(a_im * dt)
    den = a_re * a_re + a_im * a_im
    nr = ab_re - 1.0
    f_re = (nr * a_re + ab_im * a_im) / den
    f_im = (ab_im * a_re - nr * a_im) / den
    bb_re = f_re[..., None] * b_re - f_im[..., None] * b_im
    bb_im = f_re[..., None] * b_im + f_im[..., None] * b_re
    return ab_re, ab_im, bb_re, bb_im


def complex_affine_combine(e1, e2):
    a1r, a1i, b1r, b1i = e1
    a2r, a2i, b2r, b2i = e2
    return (a2r * a1r - a2i * a1i, a2r * a1i + a2i * a1r,
            a2r * b1r - a2i * b1i + b2r, a2r * b1i + a2i * b1r + b2i)


def s5_chunked(inputs, state0, a_re, a_im, log_dt, b_re, b_im, c_re, c_im):
    (u,) = inputs
    bsz, L, _ = u.shape
    C = S5_CHUNK
    ab_re, ab_im, bb_re, bb_im = s5_discretise(a_re, a_im, log_dt, b_re, b_im)
    ug = u.reshape(bsz, L // C, C, S5_GROUPS, S5_GROUP).transpose(1, 2, 0, 3, 4)
    a_seq_re = jnp.broadcast_to(ab_re, (C, bsz, S5_GROUPS, S5_STATE))
    a_seq_im = jnp.broadcast_to(ab_im, (C, bsz, S5_GROUPS, S5_STATE))

    def step(state, ub):
        s_re, s_im = state
        bu_re = jnp.einsum('tbgc,gpc->tbgp', ub, bb_re)
        bu_im = jnp.einsum('tbgc,gpc->tbgp', ub, bb_im)
        pr, pi, xr, xi = lax.associative_scan(complex_affine_combine, (a_seq_re, a_seq_im, bu_re, bu_im), axis=0)
        xr, xi = xr + pr * s_re - pi * s_im, xi + pr * s_im + pi * s_re
        y = jnp.einsum('tbgp,gcp->tbgc', xr, c_re) - jnp.einsum('tbgp,gcp->tbgc', xi, c_im)
        return (xr[-1], xi[-1]), y

    state, y = lax.scan(step, state0, ug)
    return y.transpose(2, 0, 1, 3, 4).reshape(bsz, L, S5_GROUPS * S5_GROUP), state


def s5_branch(uc, ul, a_re, a_im, log_dt, b_re, b_im, c_re, c_im, d, glu_w, glu_b):
    f32 = jnp.float32
    p = [t.astype(f32) for t in (a_re, a_im, log_dt, b_re, b_im, c_re, c_im)]
    dir_fn = lambda i: functools.partial(s5_chunked, a_re=p[0][i], a_im=p[1][i], log_dt=p[2][i],
                                         b_re=p[3][i], b_im=p[4][i], c_re=p[5][i], c_im=p[6][i])
    uc, ul = uc.astype(f32), ul.astype(f32)
    zeros = jnp.zeros((uc.shape[0], S5_GROUPS, S5_STATE), f32)
    yc, yl = bidirectional_prefix_scan(dir_fn(0), dir_fn(1), (uc,), (ul,), (uc,), (ul,), (zeros, zeros))
    dd, w, bb = d.astype(f32), glu_w.astype(f32), glu_b.astype(f32)

    def out(y, u):
        y = jax.nn.gelu(y + dd * u)
        return y * jax.nn.sigmoid(y @ w + bb)

    return out(yc, uc), out(yl, ul)


def gla_chunked(inputs, state0):
    q, k, v, log_a = inputs
    C = GLA_CHUNK
    idx = jnp.arange(C)
    lower = (idx[:, None] >= idx[None, :])[None, :, :, None, None]

    def step(S, blk):
        qb, kb, vb, ab = blk
        b = jnp.cumsum(ab, axis=1)
        diff = b[:, :, None] - b[:, None, :]
        dec = jnp.where(lower, jnp.exp(jnp.minimum(diff, 0.0)), 0.0)
        scores = jnp.einsum('bthd,bshd,btshd->bhts', qb, kb, dec)
        o = jnp.einsum('bhts,bshv->bthv', scores, vb)
        o = o + jnp.einsum('bthd,bhdv->bthv', qb * jnp.exp(b), S)
        b_end = b[:, -1]
        S = S * jnp.exp(b_end)[..., None] + jnp.einsum('bshd,bshv->bhdv', kb * jnp.exp(b_end[:, None] - b), vb)
        return S, o

    S, o = lax.scan(step, state0, tuple(to_chunks(a, C) for a in (q, k, v, log_a)))
    return from_chunks(o), S


def gla_branch(pc, pl, gate_w, gate_b, norm_gain):
    f32 = jnp.float32
    gw, gb = gate_w.astype(f32), gate_b.astype(f32)

    def prep(parts):
        q, k, v, g, af, ab = parts
        bsz, L = q.shape[:2]
        hk = lambda t: t.reshape(bsz, L, GLA_HEADS, GLA_DK)
        q_ = hk(q.astype(f32))
        k_ = hk(k.astype(f32)) * GLA_DK ** -0.5
        v_ = v.astype(f32).reshape(bsz, L, GLA_HEADS, GLA_DV)
        la_f = hk(jax.nn.log_sigmoid(af.astype(f32) @ gw[0] + gb[0]) / GLA_TAU)
        la_b = hk(jax.nn.log_sigmoid(ab.astype(f32) @ gw[1] + gb[1]) / GLA_TAU)
        return (q_, k_, v_, la_f), (q_, k_, v_, la_b), g

    cf, cb, gc = prep(pc)
    lf, lb, gl = prep(pl)
    state0 = jnp.zeros((gc.shape[0], GLA_HEADS, GLA_DK, GLA_DV), f32)
    oc, ol = bidirectional_prefix_scan(gla_chunked, gla_chunked, cf, lf, cb, lb, state0)
    gain = norm_gain.astype(f32).reshape(GLA_HEADS, GLA_DV)

    def out(o, g):
        on = o * lax.rsqrt(jnp.mean(o * o, axis=-1, keepdims=True) + EPS) * gain
        return jax.nn.silu(g.astype(f32)) * on.reshape(o.shape[0], o.shape[1], -1)

    return out(oc, gc), out(ol, gl)


def depthwise_conv_centered(x, w, b):
    K, C = w.shape
    y = lax.conv_general_dilated(x, w.astype(x.dtype)[:, None, :], window_strides=(1,),
                                 padding=[(K // 2, K // 2)], dimension_numbers=('NWC', 'WIO', 'NWC'),
                                 feature_group_count=C)
    return y + b.astype(x.dtype)


def hyena_filter_spectrum(L, w1, b1, w2, b2, w3, freq):
    f32 = jnp.float32
    t = jnp.linspace(0.0, 1.0, L, dtype=f32)
    w = 2.0 * math.pi * jnp.arange(L, dtype=f32) / L
    bands = jnp.linspace(1e-4, HY_BANDS - 1.0, HY_BANDS, dtype=f32)
    ph = w[:, None] * bands[None, :]
    z = jnp.concatenate([t[:, None], jnp.cos(ph), -jnp.sin(ph)], axis=-1)
    fr = freq.astype(f32)
    h = jnp.sin(fr[0] * (z @ w1.astype(f32) + b1.astype(f32)))
    h = jnp.sin(fr[1] * (h @ w2.astype(f32) + b2.astype(f32)))
    h = (h @ w3.astype(f32)).reshape(L, HY_ORDER, 2, BRANCH_W)
    deltas = jnp.abs(jnp.linspace(math.log(HY_TARGET) / HY_FAST, math.log(HY_TARGET) / HY_SLOW, BRANCH_W, dtype=f32))
    window = jnp.exp(-t[:, None] * deltas[None, :]) + HY_SHIFT
    h = h * window[:, None, None, :]
    kern = jnp.concatenate([h[:, :, 0], jnp.zeros((1, HY_ORDER, BRANCH_W), f32), jnp.flip(h[1:, :, 1], axis=0)], axis=0)
    kern = kern * lax.rsqrt(jnp.sum(kern * kern, axis=0, keepdims=True) + EPS)
    return jnp.fft.rfft(kern, axis=0)


def hyena_branch(zc, zl, short_w, short_b, w1, b1, w2, b2, w3, freq, bias):
    f32 = jnp.float32
    bias32 = bias.astype(f32)

    def run(z):
        L = z.shape[1]
        spec = hyena_filter_spectrum(L, w1, b1, w2, b2, w3, freq)
        z = depthwise_conv_centered(z, short_w, short_b).astype(f32)
        v, x1, x2 = jnp.split(z, HY_ORDER + 1, axis=-1)
        y = v
        for o, gate in enumerate((x1, x2)):
            yf = jnp.fft.irfft(jnp.fft.rfft(y, n=2 * L, axis=1) * spec[None, :, o], n=2 * L, axis=1)[:, :L]
            y = gate * (yf + y * bias32[o])
        return y

    return run(zc), run(zl)


def hybrid_mixer(xc, xl, lp, rope_cos, rope_sin, with_ctx_out):
    split_at = np.cumsum(IN_SPLITS)[:-1].tolist()
    pc = jnp.split(xc @ lp['w_in'], split_at, axis=-1)
    pl = jnp.split(xl @ lp['w_in'], split_at, axis=-1)
    ret_c, ret_l = retention_branch(pc[0:4], pl[0:4], rope_cos, rope_sin, lp['ret_decay_logit'], lp['ret_norm_gain'])
    s5_c, s5_l = s5_branch(pc[4], pl[4], lp['s5_a_re'], lp['s5_a_im'], lp['s5_log_dt'], lp['s5_b_re'], lp['s5_b_im'],
                           lp['s5_c_re'], lp['s5_c_im'], lp['s5_d'], lp['s5_glu_w'], lp['s5_glu_b'])
    gla_c, gla_l = gla_branch(pc[5:11], pl[5:11], lp['gla_gate_w'], lp['gla_gate_b'], lp['gla_norm_gain'])
    hy_c, hy_l = hyena_branch(pc[11], pl[11], lp['hy_short_w'], lp['hy_short_b'], lp['hy_w1'], lp['hy_b1'],
                              lp['hy_w2'], lp['hy_b2'], lp['hy_w3'], lp['hy_freq'], lp['hy_bias'])

    def merge(xn, branches):
        g = jax.nn.sigmoid(xn @ lp['merge_w'] + lp['merge_b'])
        g = g.reshape(xn.shape[0], xn.shape[1], N_BRANCH, D_MODEL)
        acc = None
        for i, y in enumerate(branches):
            term = g[:, :, i] * (y.astype(xn.dtype) @ lp['branch_w'][i])
            acc = term if acc is None else acc + term
        return acc @ lp['w_out']

    y_l = merge(xl, (ret_l, s5_l, gla_l, hy_l))
    y_c = merge(xc, (ret_c, s5_c, gla_c, hy_c)) if with_ctx_out else None
    return y_c, y_l


def setup_inputs(seed: int = 0) -> dict:
    key = jax.random.key(seed)
    ks = iter(jax.random.split(key, 48))
    f32 = jnp.float32
    nrm = lambda shape, std: std * jax.random.normal(next(ks), shape, f32)
    gain = lambda shape: 1.0 + 0.02 * jax.random.normal(next(ks), shape, f32)
    D, G, P = D_MODEL, S5_GROUPS, S5_STATE
    ret_logit0 = jnp.log(2.0 ** (5.0 + jnp.arange(RET_HEADS, dtype=f32)) - 1.0)
    return {
        'x': nrm((BATCH, SEQ, D), 1.0),
        'c': nrm((BATCH, D), 1.0),
        'ctx': nrm((BATCH, CTX_LEN, D), 1.0),
        'c_ctx': nrm((D,), 1.0),
        'w_ada': nrm((DEPTH, D, N_MOD * D), 0.5 * D ** -0.5),
        'b_ada': nrm((DEPTH, N_MOD * D), 0.02),
        'norm_gain': gain((DEPTH, 6, D)),
        'ffn_w_gate': nrm((DEPTH, 2, D, D_FF), D ** -0.5),
        'ffn_w_up': nrm((DEPTH, 2, D, D_FF), D ** -0.5),
        'ffn_w_down': nrm((DEPTH, 2, D_FF, D), D_FF ** -0.5),
        'w_in': nrm((DEPTH, D, IN_COLS), D ** -0.5),
        'ret_decay_logit': ret_logit0 + nrm((DEPTH, 2, RET_HEADS), 0.05),
        'ret_norm_gain': gain((DEPTH, RET_HEADS * RET_DV)),
        's5_a_re': -0.5 + nrm((DEPTH, 2, G, P), 0.01),
        's5_a_im': math.pi * jnp.arange(P, dtype=f32) + nrm((DEPTH, 2, G, P), 0.01),
        's5_log_dt': jax.random.uniform(next(ks), (DEPTH, 2, G), f32, math.log(1e-3), math.log(1e-1)),
        's5_b_re': nrm((DEPTH, 2, G, P, S5_GROUP), (2.0 * S5_GROUP) ** -0.5),
        's5_b_im': nrm((DEPTH, 2, G, P, S5_GROUP), (2.0 * S5_GROUP) ** -0.5),
        's5_c_re': nrm((DEPTH, 2, G, S5_GROUP, P), 1.0),
        's5_c_im': nrm((DEPTH, 2, G, S5_GROUP, P), 1.0),
        's5_d': nrm((DEPTH, BRANCH_W), 0.5),
        's5_glu_w': nrm((DEPTH, BRANCH_W, BRANCH_W), BRANCH_W ** -0.5),
        's5_glu_b': nrm((DEPTH, BRANCH_W), 0.02),
        'gla_gate_w': nrm((DEPTH, 2, GLA_RANK, GLA_HEADS * GLA_DK), GLA_RANK ** -0.5),
        'gla_gate_b': nrm((DEPTH, 2, GLA_HEADS * GLA_DK), 0.1),
        'gla_norm_gain': gain((DEPTH, GLA_HEADS * GLA_DV)),
        'hy_short_w': nrm((DEPTH, HY_SHORT, (HY_ORDER + 1) * BRANCH_W), HY_SHORT ** -0.5),
        'hy_short_b': nrm((DEPTH, (HY_ORDER + 1) * BRANCH_W), 0.02),
        'hy_w1': nrm((DEPTH, HY_EMB, HY_HIDDEN), HY_EMB ** -0.5),
        'hy_b1': nrm((DEPTH, HY_HIDDEN), 0.1),
        'hy_w2': nrm((DEPTH, HY_HIDDEN, HY_HIDDEN), HY_HIDDEN ** -0.5),
        'hy_b2': nrm((DEPTH, HY_HIDDEN), 0.1),
        'hy_w3': nrm((DEPTH, HY_HIDDEN, HY_ORDER * 2 * BRANCH_W), HY_HIDDEN ** -0.5),
        'hy_freq': 1.0 + nrm((DEPTH, 2, HY_HIDDEN), 0.1),
        'hy_bias': nrm((DEPTH, HY_ORDER, BRANCH_W), 0.5),
        'branch_w': nrm((DEPTH, N_BRANCH, BRANCH_W, D), BRANCH_W ** -0.5),
        'merge_w': nrm((DEPTH, D, N_BRANCH * D), D ** -0.5),
        'merge_b': nrm((DEPTH, N_BRANCH * D), 0.02),
        'w_out': nrm((DEPTH, D, D), D ** -0.5),
    }


def reference(x, c, ctx, c_ctx, w_ada, b_ada, norm_gain, ffn_w_gate, ffn_w_up, ffn_w_down, w_in,
              ret_decay_logit, ret_norm_gain, s5_a_re, s5_a_im, s5_log_dt, s5_b_re, s5_b_im, s5_c_re, s5_c_im,
              s5_d, s5_glu_w, s5_glu_b, gla_gate_w, gla_gate_b, gla_norm_gain, hy_short_w, hy_short_b,
              hy_w1, hy_b1, hy_w2, hy_b2, hy_w3, hy_freq, hy_bias, branch_w, merge_w, merge_b, w_out):
    rows = x.shape[1] // GRID_W
    rope_cos, rope_sin = axial_rope(rows)
    hl, hc = x, ctx
    for l in range(DEPTH):
        keep_ctx = l < DEPTH - 1
        ml = (jax.nn.silu(c) @ w_ada[l] + b_ada[l]).reshape(c.shape[0], N_MOD, D_MODEL).transpose(1, 0, 2)[:, :, None, :]
        mc = (jax.nn.silu(c_ctx) @ w_ada[l] + b_ada[l]).reshape(N_MOD, 1, 1, D_MODEL)
        ng = norm_gain[l]
        ffn1 = functools.partial(swiglu, w_gate=ffn_w_gate[l, 0], w_up=ffn_w_up[l, 0], w_down=ffn_w_down[l, 0])
        ffn2 = functools.partial(swiglu, w_gate=ffn_w_gate[l, 1], w_up=ffn_w_up[l, 1], w_down=ffn_w_down[l, 1])
        lp = dict(w_in=w_in[l], ret_decay_logit=ret_decay_logit[l], ret_norm_gain=ret_norm_gain[l],
                  s5_a_re=s5_a_re[l], s5_a_im=s5_a_im[l], s5_log_dt=s5_log_dt[l], s5_b_re=s5_b_re[l],
                  s5_b_im=s5_b_im[l], s5_c_re=s5_c_re[l], s5_c_im=s5_c_im[l], s5_d=s5_d[l],
                  s5_glu_w=s5_glu_w[l], s5_glu_b=s5_glu_b[l], gla_gate_w=gla_gate_w[l], gla_gate_b=gla_gate_b[l],
                  gla_norm_gain=gla_norm_gain[l], hy_short_w=hy_short_w[l], hy_short_b=hy_short_b[l],
                  hy_w1=hy_w1[l], hy_b1=hy_b1[l], hy_w2=hy_w2[l], hy_b2=hy_b2[l], hy_w3=hy_w3[l],
                  hy_freq=hy_freq[l], hy_bias=hy_bias[l], branch_w=branch_w[l], merge_w=merge_w[l],
                  merge_b=merge_b[l], w_out=w_out[l])
        hl = hl + 0.5 * ml[2] * rms_norm(ffn1(modulate(hl, ng[0], ml[0], ml[1])), ng[1])
        hc = hc + 0.5 * mc[2] * rms_norm(ffn1(modulate(hc, ng[0], mc[0], mc[1])), ng[1])
        yc, yl = hybrid_mixer(modulate(hc, ng[2], mc[3], mc[4]), modulate(hl, ng[2], ml[3], ml[4]),
                              lp, rope_cos, rope_sin, keep_ctx)
        hl = hl + ml[5] * rms_norm(yl, ng[3])
        hl = hl + 0.5 * ml[8] * rms_norm(ffn2(modulate(hl, ng[4], ml[6], ml[7])), ng[5])
        if keep_ctx:
            hc = hc + mc[5] * rms_norm(yc, ng[3])
            hc = hc + 0.5 * mc[8] * rms_norm(ffn2(modulate(hc, ng[4], mc[6], mc[7])), ng[5])
    return hl
```

```python
import functools
import math

import jax
import jax.numpy as jnp
import numpy as np
from jax import lax
from jax.experimental import pallas as pl
from jax.experimental.pallas import tpu as pltpu

F32 = jnp.float32
BF16 = jnp.bfloat16

D_MODEL = 2048
N_MOD = 9
D_FF = 5632
EPS = 1e-6
GRID_W = 64
BRANCH_W = 1024
N_BRANCH = 4

RET_HEADS, RET_DK, RET_DV, RET_CHUNK = 4, 128, 256, 128
ROPE_BASE = 10000.0
S5_GROUP, S5_GROUPS, S5_STATE = 16, 64, 64
GLA_HEADS, GLA_DK, GLA_DV, GLA_RANK, GLA_TAU, GLA_CHUNK = 4, 128, 256, 16, 16.0, 64
GLA_SUB = 16
HY_ORDER, HY_SHORT, HY_BANDS, HY_HIDDEN = 2, 3, 16, 64
HY_TARGET, HY_FAST, HY_SLOW, HY_SHIFT = 1e-2, 0.3, 1.5, 0.05

LANES = 128
ROW_TILE = 512
VMEM_LIMIT = 56 * 1024 * 1024

P_COLS = 10368
COL_RET_Q, COL_RET_K, COL_RET_V, COL_RET_G = 0, 512, 1024, 2048
COL_S5 = 3072
COL_GLA_Q, COL_GLA_K, COL_GLA_V, COL_GLA_G = 4096, 4608, 5120, 6144
COL_HY = 7168
COL_LR = 10240

S5_T = 16
S5_BLK = LANES // S5_GROUP
S5_NBLK = S5_GROUPS // S5_BLK
S5_W = S5_BLK * S5_STATE

FFT_N2 = 128
FFT_MIN_LEN = 2048


def _cparams(sem):
    return pltpu.CompilerParams(dimension_semantics=sem, vmem_limit_bytes=VMEM_LIMIT)


def _dot(a, b):
    return jnp.dot(a, b, preferred_element_type=F32)


def _dot_nt(a, b):
    return lax.dot_general(a, b, (((1,), (1,)), ((), ())), preferred_element_type=F32)


def _dot_tn(a, b):
    return lax.dot_general(a, b, (((0,), (0,)), ((), ())), preferred_element_type=F32)


def _split2(x):
    hi = x.astype(BF16)
    lo = (x - hi.astype(F32)).astype(BF16)
    return hi, lo


def _split3(x):
    hi = x.astype(BF16)
    r = x - hi.astype(F32)
    mid = r.astype(BF16)
    lo = (r - mid.astype(F32)).astype(BF16)
    return hi, mid, lo


def _mm3(a, b):
    return _dot(a[0], b[0]) + (_dot(a[0], b[1]) + _dot(a[1], b[0]))


def _cmm3(ar, ai, br, bi):
    return _mm3(ar, br) - _mm3(ai, bi), _mm3(ar, bi) + _mm3(ai, br)


def _rms(x, gain):
    return x * lax.rsqrt(jnp.mean(x * x, axis=-1, keepdims=True) + EPS) * gain


def _silu(x):
    return x * jax.nn.sigmoid(x)


def _log_sigmoid(x):
    return jnp.minimum(x, 0.0) - jnp.log1p(jnp.exp(-jnp.abs(x)))


def _mod_index(i, lat_tiles, tiles_per_batch):
    return jnp.where(i < lat_tiles, i // tiles_per_batch, lat_tiles // tiles_per_batch)


def _ada_kernel(c_ref, w_ref, b_ref, o_ref):
    cs = c_ref[...]
    o_ref[...] = _dot(_silu(cs).astype(BF16), w_ref[...].astype(BF16)) + b_ref[...]


def _ada(cond, w_ada, b_ada):
    depth, d, n = w_ada.shape
    tn = 1024
    return pl.pallas_call(
        _ada_kernel,
        grid=(depth, n // tn),
        in_specs=[pl.BlockSpec((8, d), lambda l, j: (0, 0)),
                  pl.BlockSpec((None, d, tn), lambda l, j: (l, 0, j)),
                  pl.BlockSpec((None, 1, tn), lambda l, j: (l, 0, j))],
        out_specs=pl.BlockSpec((None, 8, tn), lambda l, j: (l, 0, j)),
        out_shape=jax.ShapeDtypeStruct((depth, 8, n), F32),
        compiler_params=_cparams(("parallel", "parallel")),
        name="adaln",
    )(cond, w_ada, b_ada.reshape(depth, 1, n))


def _ffn_kernel(h_ref, mod_ref, ng_ref, wg_ref, wu_ref, wd_ref, o_ref, xn_ref, acc_ref, *, mbase, gbase):
    j = pl.program_id(1)

    @pl.when(j == 0)
    def _():
        x = h_ref[...]
        xn = _rms(x, ng_ref[gbase:gbase + 1, :]) * (1.0 + mod_ref[mbase + 1:mbase + 2, :]) + mod_ref[mbase:mbase + 1, :]
        xn_ref[...] = xn.astype(BF16)
        acc_ref[...] = jnp.zeros_like(acc_ref)

    xn = xn_ref[...]
    a = _silu(_dot(xn, wg_ref[...])) * _dot(xn, wu_ref[...])
    acc_ref[...] += _dot(a.astype(BF16), wd_ref[...])

    @pl.when(j == pl.num_programs(1) - 1)
    def _():
        r = _rms(acc_ref[...], ng_ref[gbase + 1:gbase + 2, :])
        o_ref[...] = h_ref[...] + 0.5 * mod_ref[mbase + 2:mbase + 3, :] * r


def _ffn(h, mods, ng, wg, wu, wd, *, mbase, gbase, lat_tiles, tiles_per_batch):
    r, d = h.shape
    f = wg.shape[1]
    tm, tf = ROW_TILE, 512
    midx = lambda i, j: (_mod_index(i, lat_tiles, tiles_per_batch), 0, 0)
    return pl.pallas_call(
        functools.partial(_ffn_kernel, mbase=mbase, gbase=gbase),
        grid=(r // tm, f // tf),
        in_specs=[pl.BlockSpec((tm, d), lambda i, j: (i, 0)),
                  pl.BlockSpec((None, N_MOD, d), midx),
                  pl.BlockSpec((6, d), lambda i, j: (0, 0)),
                  pl.BlockSpec((d, tf), lambda i, j: (0, j)),
                  pl.BlockSpec((d, tf), lambda i, j: (0, j)),
                  pl.BlockSpec((tf, d), lambda i, j: (j, 0))],
        out_specs=pl.BlockSpec((tm, d), lambda i, j: (i, 0)),
        out_shape=jax.ShapeDtypeStruct((r, d), F32),
        scratch_shapes=[pltpu.VMEM((tm, d), BF16), pltpu.VMEM((tm, d), F32)],
        compiler_params=_cparams(("parallel", "arbitrary")),
        name="ffn",
    )(h, mods, ng, wg, wu, wd)


def _inproj_kernel(h_ref, mod_ref, ng_ref, w_ref, o_ref, xn_ref):
    @pl.when(pl.program_id(1) == 0)
    def _():
        xn = _rms(h_ref[...], ng_ref[2:3, :]) * (1.0 + mod_ref[4:5, :]) + mod_ref[3:4, :]
        xn_ref[...] = xn.astype(BF16)

    o_ref[...] = _dot(xn_ref[...], w_ref[...])


def _inproj(h, mods, ng, w, *, lat_tiles, tiles_per_batch):
    r, d = h.shape
    n = w.shape[1]
    tm, tn = ROW_TILE, 1152
    midx = lambda i, j: (_mod_index(i, lat_tiles, tiles_per_batch), 0, 0)
    return pl.pallas_call(
        _inproj_kernel,
        grid=(r // tm, n // tn),
        in_specs=[pl.BlockSpec((tm, d), lambda i, j: (i, 0)),
                  pl.BlockSpec((None, N_MOD, d), midx),
                  pl.BlockSpec((6, d), lambda i, j: (0, 0)),
                  pl.BlockSpec((d, tn), lambda i, j: (0, j))],
        out_specs=pl.BlockSpec((tm, tn), lambda i, j: (i, j)),
        out_shape=jax.ShapeDtypeStruct((r, n), F32),
        scratch_shapes=[pltpu.VMEM((tm, d), BF16)],
        compiler_params=_cparams(("parallel", "arbitrary")),
        name="inproj",
    )(h, mods, ng, w)


def _chunk_row_block(b, j, *, lat_chunks, ctx_chunks, reverse, batch):
    ctx0 = batch * lat_chunks + b * ctx_chunks
    lat0 = b * lat_chunks
    jl = j - ctx_chunks
    if reverse:
        return jnp.where(j < ctx_chunks, ctx0 + (ctx_chunks - 1 - j), lat0 + (lat_chunks - 1 - jl))
    return jnp.where(j < ctx_chunks, ctx0 + j, lat0 + jl)


def _ret_kernel(*refs, reverse):
    if reverse:
        dl_ref, q_ref, k_ref, v_ref, cos_ref, sin_ref, of_ref, g_ref, gain_ref, o_ref, s_ref = refs
    else:
        dl_ref, q_ref, k_ref, v_ref, cos_ref, sin_ref, o_ref, s_ref = refs
    C = RET_CHUNK

    @pl.when(pl.program_id(1) == 0)
    def _():
        s_ref[...] = jnp.zeros_like(s_ref)

    cos, sin = cos_ref[...], sin_ref[...]
    ti = lax.broadcasted_iota(jnp.int32, (C, C), 0).astype(F32)
    si = lax.broadcasted_iota(jnp.int32, (C, C), 1).astype(F32)
    tr = lax.broadcasted_iota(jnp.int32, (C, RET_DK), 0).astype(F32)
    rel = (si - ti) if reverse else (ti - si)
    scale = RET_DK ** -0.5
    for h in range(RET_HEADS):
        lg_cc = _log_sigmoid(jnp.full((C, C), dl_ref[h], F32))
        lg_cd = _log_sigmoid(jnp.full((C, RET_DK), dl_ref[h], F32))
        d_intra = jnp.where(rel >= 0, jnp.exp(lg_cc * jnp.maximum(rel, 0.0)), 0.0)
        if reverse:
            d_q = jnp.exp((C - tr) * lg_cd)
            d_k = jnp.exp(tr * lg_cd)
        else:
            d_q = jnp.exp((tr + 1.0) * lg_cd)
            d_k = jnp.exp((C - 1.0 - tr) * lg_cd)
        d_c = jnp.exp(C * lg_cd[0:1, 0:1])

        qh = q_ref[:, h * RET_DK:(h + 1) * RET_DK]
        kh = k_ref[:, h * RET_DK:(h + 1) * RET_DK]
        vh = v_ref[:, h * RET_DV:(h + 1) * RET_DV].astype(BF16)
        qr = qh * cos + pltpu.roll(qh, RET_DK // 2, axis=1) * sin
        kr = (kh * cos + pltpu.roll(kh, RET_DK // 2, axis=1) * sin) * scale

        scores = _dot_nt(qr.astype(BF16), kr.astype(BF16)) * d_intra
        s_old = s_ref[h]
        o = _dot(scores.astype(BF16), vh) + _dot((qr * d_q).astype(BF16), s_old.astype(BF16))
        s_ref[h] = s_old * d_c + _dot_tn((kr * d_k).astype(BF16), vh)

        if reverse:
            o = o + of_ref[:, h * RET_DV:(h + 1) * RET_DV]
            mu = jnp.mean(o, axis=-1, keepdims=True)
            oc = o - mu
            var = jnp.mean(oc * oc, axis=-1, keepdims=True)
            on = oc * lax.rsqrt(var + EPS) * gain_ref[:, h * RET_DV:(h + 1) * RET_DV]
            o = _silu(g_ref[:, h * RET_DV:(h + 1) * RET_DV]) * on
        o_ref[:, h * RET_DV:(h + 1) * RET_DV] = o


def _retention(p, cos_t, sin_t, decay_logit, norm_gain, *, batch, lat_len, ctx_len):
    r = p.shape[0]
    C = RET_CHUNK
    lat_chunks, ctx_chunks = lat_len // C, ctx_len // C
    steps = lat_chunks + ctx_chunks

    def call(reverse, extra):
        rb = functools.partial(_chunk_row_block, lat_chunks=lat_chunks, ctx_chunks=ctx_chunks,
                               reverse=reverse, batch=batch)
        row = lambda cb: (lambda b, j: (rb(b, j), cb))
        in_specs = [pl.BlockSpec(memory_space=pltpu.SMEM),
                    pl.BlockSpec((C, 512), row(COL_RET_Q // 512)),
                    pl.BlockSpec((C, 512), row(COL_RET_K // 512)),
                    pl.BlockSpec((C, 1024), row(COL_RET_V // 1024)),
                    pl.BlockSpec((C, RET_DK), row(0)),
                    pl.BlockSpec((C, RET_DK), row(0))]
        args = [decay_logit[1 if reverse else 0], p, p, p, cos_t, sin_t]
        if reverse:
            in_specs += [pl.BlockSpec((C, 1024), row(0)),
                         pl.BlockSpec((C, 1024), row(COL_RET_G // 1024)),
                         pl.BlockSpec((1, 1024), lambda b, j: (0, 0))]
            args += [extra, p, norm_gain.reshape(1, -1)]
        return pl.pallas_call(
            functools.partial(_ret_kernel, reverse=reverse),
            grid=(batch, steps),
            in_specs=in_specs,
            out_specs=pl.BlockSpec((C, 1024), row(0)),
            out_shape=jax.ShapeDtypeStruct((r, 1024), F32),
            scratch_shapes=[pltpu.VMEM((RET_HEADS, RET_DK, RET_DV), F32)],
            compiler_params=_cparams(("parallel", "arbitrary")),
            name="retention_bwd" if reverse else "retention_fwd",
        )(*args)

    o_f = call(False, None)
    return call(True, o_f)


def _gla_kernel(*refs, reverse):
    if reverse:
        q_ref, k_ref, v_ref, a_ref, gw_ref, gb_ref, of_ref, g_ref, gain_ref, o_ref, s_ref = refs
    else:
        q_ref, k_ref, v_ref, a_ref, gw_ref, gb_ref, o_ref, s_ref = refs
    C, SB = GLA_CHUNK, GLA_SUB

    @pl.when(pl.program_id(1) == 0)
    def _():
        s_ref[...] = jnp.zeros_like(s_ref)

    la = _log_sigmoid(_dot(a_ref[...].astype(BF16), gw_ref[...]) + gb_ref[...]) / GLA_TAU
    la3 = _split3(la)
    ti = lax.broadcasted_iota(jnp.int32, (C, C), 0)
    si = lax.broadcasted_iota(jnp.int32, (C, C), 1)
    tri = jnp.where((si >= ti) if reverse else (si <= ti), 1.0, 0.0).astype(BF16)
    bcum = _dot(tri, la3[0]) + (_dot(tri, la3[1]) + _dot(tri, la3[2]))
    ones = jnp.ones((C, LANES), BF16)
    scale = GLA_DK ** -0.5
    end = 0 if reverse else C - 1

    for h in range(GLA_HEADS):
        sl = slice(h * GLA_DK, (h + 1) * GLA_DK)
        b = bcum[:, sl]
        q = q_ref[:, sl]
        k = k_ref[:, sl] * scale
        v = v_ref[:, h * GLA_DV:(h + 1) * GLA_DV].astype(BF16)
        s_old = s_ref[h]
        b_end = b[end:end + 1, :]
        o_inter = _dot((q * jnp.exp(b)).astype(BF16), s_old.astype(BF16))
        col = _dot_tn(la3[0][:, sl], ones) + (_dot_tn(la3[1][:, sl], ones) + _dot_tn(la3[2][:, sl], ones))
        dec_col = jnp.exp(jnp.concatenate([col, col], axis=1))
        s_ref[h] = s_old * dec_col + _dot_tn((k * jnp.exp(b_end - b)).astype(BF16), v)

        srow = lax.broadcasted_iota(jnp.int32, (C, GLA_DK), 0)
        for blk in range(C // SB):
            r0 = blk * SB
            ref_row = b[r0 + SB - 1:r0 + SB, :] if reverse else b[r0:r0 + 1, :]
            reach = (srow >= r0) if reverse else (srow < r0 + SB)
            qt = (q[r0:r0 + SB] * jnp.exp(b[r0:r0 + SB] - ref_row)).astype(BF16)
            kt = (k * jnp.exp(jnp.where(reach, ref_row - b, 0.0))).astype(BF16)
            sc = _dot_nt(qt, kt)
            tt = lax.broadcasted_iota(jnp.int32, sc.shape, 0) + r0
            ss = lax.broadcasted_iota(jnp.int32, sc.shape, 1)
            sc = jnp.where((ss >= tt) if reverse else (ss <= tt), sc, 0.0)
            o = _dot(sc.astype(BF16), v) + o_inter[r0:r0 + SB]
            cs = slice(h * GLA_DV, (h + 1) * GLA_DV)
            if reverse:
                o = o + of_ref[r0:r0 + SB, cs]
                on = o * lax.rsqrt(jnp.mean(o * o, axis=-1, keepdims=True) + EPS) * gain_ref[:, cs]
                o = _silu(g_ref[r0:r0 + SB, cs]) * on
            o_ref[r0:r0 + SB, cs] = o


def _gla(p, gate_w, gate_b, norm_gain, *, batch, lat_len, ctx_len):
    r = p.shape[0]
    C = GLA_CHUNK
    lat_chunks, ctx_chunks = lat_len // C, ctx_len // C
    steps = lat_chunks + ctx_chunks
    hk = GLA_HEADS * GLA_DK

    def call(reverse, extra):
        d = 1 if reverse else 0
        rb = functools.partial(_chunk_row_block, lat_chunks=lat_chunks, ctx_chunks=ctx_chunks,
                               reverse=reverse, batch=batch)
        row = lambda cb: (lambda b, j: (rb(b, j), cb))
        gw = jnp.zeros((LANES, hk), F32).at[d * GLA_RANK:(d + 1) * GLA_RANK].set(gate_w[d]).astype(BF16)
        in_specs = [pl.BlockSpec((C, 512), row(COL_GLA_Q // 512)),
                    pl.BlockSpec((C, 512), row(COL_GLA_K // 512)),
                    pl.BlockSpec((C, 1024), row(COL_GLA_V // 1024)),
                    pl.BlockSpec((C, LANES), row(COL_LR // LANES)),
                    pl.BlockSpec((LANES, hk), lambda b, j: (0, 0)),
                    pl.BlockSpec((1, hk), lambda b, j: (0, 0))]
        args = [p, p, p, p, gw, gate_b[d].reshape(1, hk)]
        if reverse:
            in_specs += [pl.BlockSpec((C, 1024), row(0)),
                         pl.BlockSpec((C, 1024), row(COL_GLA_G // 1024)),
                         pl.BlockSpec((1, 1024), lambda b, j: (0, 0))]
            args += [extra, p, norm_gain.reshape(1, -1)]
        return pl.pallas_call(
            functools.partial(_gla_kernel, reverse=reverse),
            grid=(batch, steps),
            in_specs=in_specs,
            out_specs=pl.BlockSpec((C, 1024), row(0)),
            out_shape=jax.ShapeDtypeStruct((r, 1024), F32),
            scratch_shapes=[pltpu.VMEM((GLA_HEADS, GLA_DK, GLA_DV), F32)],
            compiler_params=_cparams(("parallel", "arbitrary")),
            name="gla_bwd" if reverse else "gla_fwd",
        )(*args)

    o_f = call(False, None)
    return call(True, o_f)


def _s5_tables(a_re, a_im, log_dt, b_re, b_im, c_re, c_im):
    T, G, P, K = S5_T, S5_GROUPS, S5_STATE, S5_GROUP
    dt = jnp.exp(log_dt)[:, None]
    mag = jnp.exp(a_re * dt)
    ab_re, ab_im = mag * jnp.cos(a_im * dt), mag * jnp.sin(a_im * dt)
    den = a_re * a_re + a_im * a_im
    nr = ab_re - 1.0
    f_re = (nr * a_re + ab_im * a_im) / den
    f_im = (ab_im * a_re - nr * a_im) / den
    bb_re = f_re[..., None] * b_re - f_im[..., None] * b_im
    bb_im = f_re[..., None] * b_im + f_im[..., None] * b_re

    def power(n):
        n = jnp.asarray(n, F32)[:, None, None]
        m = jnp.exp(n * (a_re * dt))
        return m * jnp.cos(n * (a_im * dt)), m * jnp.sin(n * (a_im * dt))

    eye = jnp.eye(S5_BLK, dtype=F32)

    def blockdiag_in(x):
        lead = x.shape[:-3]
        x = x.reshape(*lead, S5_NBLK, S5_BLK, P, K)
        y = jnp.einsum('...bgpc,gh->...bgchp', x, eye)
        return y.reshape(*lead, S5_NBLK, LANES, S5_W)

    pr, pi = power(np.arange(T - 1, -1, -1))
    w_re = pr[..., None] * bb_re - pi[..., None] * bb_im
    w_im = pr[..., None] * bb_im + pi[..., None] * bb_re
    w_end = jnp.concatenate([blockdiag_in(w_re), blockdiag_in(w_im)], axis=-1).astype(BF16)
    w_end = jnp.moveaxis(w_end, 1, 0)

    qr, qi = power(np.arange(T))
    lb_re = qr[..., None] * bb_re - qi[..., None] * bb_im
    lb_im = qr[..., None] * bb_im + qi[..., None] * bb_re
    kj = jnp.einsum('gop,jgpi->jgio', c_re, lb_re) - jnp.einsum('gop,jgpi->jgio', c_im, lb_im)
    kj = kj.reshape(T, S5_NBLK, S5_BLK, K, K)
    kj = jnp.einsum('jbgio,gh->jbgiho', kj, eye).reshape(T, S5_NBLK, LANES, LANES).astype(BF16)
    kj = jnp.moveaxis(kj, 1, 0)

    def blockdiag_out(x):
        x = x.reshape(S5_NBLK, S5_BLK, K, P)
        return jnp.einsum('bgcp,gh->bgphc', x, eye).reshape(S5_NBLK, S5_W, LANES)

    c_out = jnp.concatenate([blockdiag_out(c_re), -blockdiag_out(c_im)], axis=1).astype(BF16)

    sr, si = power(np.arange(1, T + 1))
    lam_pow = jnp.concatenate([sr.reshape(T, S5_NBLK, 1, S5_W), si.reshape(T, S5_NBLK, 1, S5_W)], axis=-1)
    lam_pow = jnp.moveaxis(lam_pow, 1, 0)
    return w_end, kj, c_out, lam_pow


def _s5_kernel(u_ref, x0_ref, wend_ref, kj_ref, cout_ref, lam_ref, y_ref, xf_ref,
               ubuf_ref, sloc_ref, xin_ref, x_ref, *, reverse, n):
    T, W = S5_T, S5_W

    @pl.when(pl.program_id(2) == 0)
    def _():
        x_ref[...] = x0_ref[...]

    off = lambda r: (T - 1 - r) if reverse else r
    for r in range(T):
        ubuf_ref[r] = u_ref[pl.ds(off(r), n, stride=T), :].astype(BF16)

    sloc = _dot(ubuf_ref[0], wend_ref[0])
    for r in range(1, T):
        sloc = sloc + _dot(ubuf_ref[r], wend_ref[r])
    sloc_ref[...] = sloc

    lt = lam_ref[T - 1]
    ltr, lti = lt[:, :W], lt[:, W:]

    def body(i, x):
        c = (n - 1 - i) if reverse else i
        xin_ref[pl.ds(c, 1), :] = x
        s = sloc_ref[pl.ds(c, 1), :]
        xr, xi = x[:, :W], x[:, W:]
        return jnp.concatenate([ltr * xr - lti * xi + s[:, :W], ltr * xi + lti * xr + s[:, W:]], axis=1)

    x_fin = lax.fori_loop(0, n, body, x_ref[...])
    x_ref[...] = x_fin
    xf_ref[...] = x_fin

    xin = xin_ref[...]
    xr, xi = xin[:, :W], xin[:, W:]
    cout = cout_ref[...]
    for r in range(T):
        lp = lam_ref[r]
        pr, pi = lp[:, :W], lp[:, W:]
        z = jnp.concatenate([pr * xr - pi * xi, pr * xi + pi * xr], axis=1)
        zh, zl = _split2(z)
        y = _dot(zh, cout) + _dot(zl, cout)
        for j in range(r + 1):
            y = y + _dot(ubuf_ref[r - j], kj_ref[j])
        y_ref[pl.ds(off(r), n, stride=T), :] = y


def _s5_scan(p, tables, x0, *, reverse, row0, rows_per_batch, tile, batch):
    w_end, kj, c_out, lam_pow = tables
    T, W = S5_T, S5_W
    n = tile // T
    steps = rows_per_batch // tile
    blk0 = row0 // tile

    def urow(kb, b, j):
        jj = (steps - 1 - j) if reverse else j
        return blk0 + b * steps + jj

    y, xf = pl.pallas_call(
        functools.partial(_s5_kernel, reverse=reverse, n=n),
        grid=(S5_NBLK, batch, steps),
        in_specs=[pl.BlockSpec((tile, LANES), lambda kb, b, j: (urow(kb, b, j), COL_S5 // LANES + kb)),
                  pl.BlockSpec((None, 1, 2 * W), lambda kb, b, j: (b * S5_NBLK + kb, 0, 0)),
                  pl.BlockSpec((None, T, LANES, 2 * W), lambda kb, b, j: (kb, 0, 0, 0)),
                  pl.BlockSpec((None, T, LANES, LANES), lambda kb, b, j: (kb, 0, 0, 0)),
                  pl.BlockSpec((None, 2 * W, LANES), lambda kb, b, j: (kb, 0, 0)),
                  pl.BlockSpec((None, T, 1, 2 * W), lambda kb, b, j: (kb, 0, 0, 0))],
        out_specs=[pl.BlockSpec((tile, LANES), lambda kb, b, j: (urow(kb, b, j) - blk0, kb)),
                   pl.BlockSpec((None, 1, 2 * W), lambda kb, b, j: (b * S5_NBLK + kb, 0, 0))],
        out_shape=[jax.ShapeDtypeStruct((batch * rows_per_batch, BRANCH_W), F32),
                   jax.ShapeDtypeStruct((batch * S5_NBLK, 1, 2 * W), F32)],
        scratch_shapes=[pltpu.VMEM((T, n, LANES), BF16), pltpu.VMEM((n, 2 * W), F32),
                        pltpu.VMEM((n, 2 * W), F32), pltpu.VMEM((1, 2 * W), F32)],
        compiler_params=_cparams(("parallel", "parallel", "arbitrary")),
        name="s5_scan",
    )(p, x0, w_end, kj, c_out, lam_pow)
    return y, xf


def _gelu_tanh(x):
    return 0.5 * x * (1.0 + jnp.tanh(math.sqrt(2.0 / math.pi) * (x + 0.044715 * (x * x * x))))


def _s5_out_kernel(yf_ref, yb_ref, u_ref, d_ref, w_ref, b_ref, o_ref):
    y = _gelu_tanh(yf_ref[...] + yb_ref[...] + d_ref[...] * u_ref[...])
    o_ref[...] = y * jax.nn.sigmoid(_dot(y.astype(BF16), w_ref[...]) + b_ref[...])


def _s5(p, lp, *, batch, lat_len, ctx_len):
    r = p.shape[0]
    lat_rows = batch * lat_len
    zeros = jnp.zeros((batch * S5_NBLK, 1, 2 * S5_W), F32)
    ys = []
    for d in range(2):
        tables = _s5_tables(*(lp[k][d] for k in ('s5_a_re', 's5_a_im', 's5_log_dt', 's5_b_re', 's5_b_im',
                                                  's5_c_re', 's5_c_im')))
        y_c, x_c = _s5_scan(p, tables, zeros, reverse=bool(d), row0=lat_rows, rows_per_batch=ctx_len,
                            tile=ctx_len, batch=batch)
        y_l, _ = _s5_scan(p, tables, x_c, reverse=bool(d), row0=0, rows_per_batch=lat_len,
                          tile=min(lat_len, 4096), batch=batch)
        ys.append(jnp.concatenate([y_l, y_c], axis=0))
    tm = ROW_TILE
    return pl.pallas_call(
        _s5_out_kernel,
        grid=(r // tm,),
        in_specs=[pl.BlockSpec((tm, BRANCH_W), lambda i: (i, 0)),
                  pl.BlockSpec((tm, BRANCH_W), lambda i: (i, 0)),
                  pl.BlockSpec((tm, BRANCH_W), lambda i: (i, COL_S5 // BRANCH_W)),
                  pl.BlockSpec((1, BRANCH_W), lambda i: (0, 0)),
                  pl.BlockSpec((BRANCH_W, BRANCH_W), lambda i: (0, 0)),
                  pl.BlockSpec((1, BRANCH_W), lambda i: (0, 0))],
        out_specs=pl.BlockSpec((tm, BRANCH_W), lambda i: (i, 0)),
        out_shape=jax.ShapeDtypeStruct((r, BRANCH_W), F32),
        compiler_params=_cparams(("parallel",)),
        name="s5_out",
    )(ys[0], ys[1], p, lp['s5_d'].reshape(1, -1), lp['s5_glu_w'].astype(BF16), lp['s5_glu_b'].reshape(1, -1))


def _hy_short_kernel(z_ref, w_ref, b_ref, o_ref):
    z = z_ref[...]
    n = z.shape[0]
    t = lax.broadcasted_iota(jnp.int32, z.shape, 0)
    zm = jnp.where(t == 0, 0.0, pltpu.roll(z, 1, axis=0))
    zp = jnp.where(t == n - 1, 0.0, pltpu.roll(z, n - 1, axis=0))
    o_ref[...] = w_ref[0:1, :] * zm + w_ref[1:2, :] * z + w_ref[2:3, :] * zp + b_ref[...]


def _hy_short(p, w, b, *, row0, seq_len, batch):
    cb = 256
    nb = BRANCH_W // cb
    blk0 = row0 // seq_len
    return pl.pallas_call(
        _hy_short_kernel,
        grid=(HY_ORDER + 1, batch, nb),
        in_specs=[pl.BlockSpec((seq_len, cb), lambda s, bb, c: (blk0 + bb, COL_HY // cb + s * nb + c)),
                  pl.BlockSpec((HY_SHORT, cb), lambda s, bb, c: (0, s * nb + c)),
                  pl.BlockSpec((1, cb), lambda s, bb, c: (0, s * nb + c))],
        out_specs=pl.BlockSpec((None, None, seq_len, cb), lambda s, bb, c: (s, bb, 0, c)),
        out_shape=jax.ShapeDtypeStruct((HY_ORDER + 1, batch, seq_len, BRANCH_W), F32),
        compiler_params=_cparams(("parallel", "parallel", "parallel")),
        name="hyena_short",
    )(p, w, b.reshape(1, -1))


def _dft(n_out, n_in, n, sign):
    k = np.arange(n_out)[:, None].astype(np.float64)
    m = np.arange(n_in)[None, :].astype(np.float64)
    ang = sign * 2.0 * np.pi * ((k * m) % n) / n
    return np.cos(ang), np.sin(ang)


def _const_split(x):
    x = jnp.asarray(x, F32)
    return _split2(x)


def _fft_fwd1_kernel(z_ref, frh_ref, frl_ref, fih_ref, fil_ref, ar_ref, ai_ref):
    zr, zi = _split2(z_ref[0]), _split2(z_ref[1])
    fr, fi = (frh_ref[...], frl_ref[...]), (fih_ref[...], fil_ref[...])
    ar, ai = _cmm3(fr, fi, zr, zi)
    ar_ref[...] = ar
    ai_ref[...] = ai


def _fft_mid_kernel(ar_ref, ai_ref, twr_ref, twi_ref, hr_ref, hi_ref, frh_ref, frl_ref, fih_ref, fil_ref,
                    br_ref, bi_ref, *, reps):
    twr = jnp.concatenate([twr_ref[...]] * reps, axis=1)
    twi = jnp.concatenate([twi_ref[...]] * reps, axis=1)
    ar, ai = ar_ref[...], ai_ref[...]
    xr, xi = ar * twr - ai * twi, ar * twi + ai * twr
    fr, fi = (frh_ref[...], frl_ref[...]), (fih_ref[...], fil_ref[...])
    sr, si = _cmm3(fr, fi, _split2(xr), _split2(xi))
    hr, hi = hr_ref[...], hi_ref[...]
    yr, yi = sr * hr - si * hi, sr * hi + si * hr
    nfi = (-fi[0], -fi[1])
    br, bi = _cmm3(fr, nfi, _split2(yr), _split2(yi))
    br_ref[...] = br * twr + bi * twi
    bi_ref[...] = bi * twr - br * twi


def _fft_inv1_kernel(br_ref, bi_ref, grh_ref, grl_ref, gih_ref, gil_ref, gate_ref, y_ref, bias_ref, o_ref, *, inv_n):
    gr, gi = (grh_ref[...], grl_ref[...]), (gih_ref[...], gil_ref[...])
    outr, outi = _cmm3(gr, gi, _split2(br_ref[...]), _split2(bi_ref[...]))
    bias = bias_ref[...]
    o_ref[0] = gate_ref[0] * (outr * inv_n + y_ref[0] * bias)
    o_ref[1] = gate_ref[1] * (outi * inv_n + y_ref[1] * bias)


def _hy_longconv_lat(y, gate, bias, spec_r, spec_i, consts, *, seq_len):
    n1 = 2 * seq_len // FFT_N2
    n1h = seq_len // FFT_N2
    C = y.shape[-1]
    cols = FFT_N2 * C
    f1, f2, g1, twr, twi = consts
    yv = y.reshape(2, n1h, cols)
    cw = C
    const = lambda shape: pl.BlockSpec(shape, lambda i: (0,) * len(shape))
    ar, ai = pl.pallas_call(
        _fft_fwd1_kernel,
        grid=(cols // cw,),
        in_specs=[pl.BlockSpec((2, n1h, cw), lambda i: (0, 0, i))] + [const((n1, n1h))] * 4,
        out_specs=[pl.BlockSpec((n1, cw), lambda i: (0, i))] * 2,
        out_shape=[jax.ShapeDtypeStruct((n1, cols), F32)] * 2,
        compiler_params=_cparams(("parallel",)),
        name="hyena_fft_fwd1",
    )(yv, *f1)
    a3 = lambda a: a.reshape(n1, FFT_N2, C)
    blk = pl.BlockSpec((None, FFT_N2, C), lambda i: (i, 0, 0))
    br, bi = pl.pallas_call(
        functools.partial(_fft_mid_kernel, reps=C // LANES),
        grid=(n1,),
        in_specs=[blk, blk,
                  pl.BlockSpec((None, FFT_N2, LANES), lambda i: (i, 0, 0)),
                  pl.BlockSpec((None, FFT_N2, LANES), lambda i: (i, 0, 0)),
                  blk, blk] + [const((FFT_N2, FFT_N2))] * 4,
        out_specs=[blk, blk],
        out_shape=[jax.ShapeDtypeStruct((n1, FFT_N2, C), F32)] * 2,
        compiler_params=_cparams(("parallel",)),
        name="hyena_fft_mid",
    )(a3(ar), a3(ai), twr, twi, spec_r, spec_i, *f2)
    out = pl.pallas_call(
        functools.partial(_fft_inv1_kernel, inv_n=1.0 / (n1 * FFT_N2)),
        grid=(cols // cw,),
        in_specs=[pl.BlockSpec((n1, cw), lambda i: (0, i))] * 2 + [const((n1h, n1))] * 4
                 + [pl.BlockSpec((2, n1h, cw), lambda i: (0, 0, i))] * 2 + [const((1, cw))],
        out_specs=pl.BlockSpec((2, n1h, cw), lambda i: (0, 0, i)),
        out_shape=jax.ShapeDtypeStruct((2, n1h, cols), F32),
        compiler_params=_cparams(("parallel",)),
        name="hyena_fft_inv1",
    )(br.reshape(n1, cols), bi.reshape(n1, cols), *g1, gate.reshape(2, n1h, cols), yv, bias.reshape(1, C))
    return out.reshape(2, seq_len, C)


def _hy_ctx_kernel(y_ref, gate_ref, bias_ref, hr_ref, hi_ref, frh_ref, frl_ref, fih_ref, fil_ref,
                   grh_ref, grl_ref, gih_ref, gil_ref, o_ref, *, inv_n):
    fr, fi = (frh_ref[...], frl_ref[...]), (fih_ref[...], fil_ref[...])
    sr, si = _cmm3(fr, fi, _split2(y_ref[0]), _split2(y_ref[1]))
    hr, hi = hr_ref[...], hi_ref[...]
    yr, yi = sr * hr - si * hi, sr * hi + si * hr
    gr, gi = (grh_ref[...], grl_ref[...]), (gih_ref[...], gil_ref[...])
    outr, outi = _cmm3(gr, gi, _split2(yr), _split2(yi))
    bias = bias_ref[...]
    o_ref[0] = gate_ref[0] * (outr * inv_n + y_ref[0] * bias)
    o_ref[1] = gate_ref[1] * (outi * inv_n + y_ref[1] * bias)


def _hy_longconv_ctx(y, gate, bias, spec_r, spec_i, consts, *, seq_len):
    C = y.shape[-1]
    cb = 256
    n = 2 * seq_len
    fwd, inv = consts
    const = lambda shape: pl.BlockSpec(shape, lambda i: (0,) * len(shape))
    data = pl.BlockSpec((2, seq_len, cb), lambda i: (0, 0, i))
    return pl.pallas_call(
        functools.partial(_hy_ctx_kernel, inv_n=1.0 / n),
        grid=(C // cb,),
        in_specs=[data, data, pl.BlockSpec((1, cb), lambda i: (0, i)),
                  pl.BlockSpec((n, cb), lambda i: (0, i)), pl.BlockSpec((n, cb), lambda i: (0, i))]
                 + [const((n, seq_len))] * 4 + [const((seq_len, n))] * 4,
        out_specs=data,
        out_shape=jax.ShapeDtypeStruct((2, seq_len, C), F32),
        compiler_params=_cparams(("parallel",)),
        name="hyena_ctx_conv",
    )(y, gate, bias.reshape(1, C), spec_r, spec_i, *fwd, *inv)


def _hyena_filter_spectrum(L, w1, b1, w2, b2, w3, freq):
    t = jnp.linspace(0.0, 1.0, L, dtype=F32)
    w = 2.0 * math.pi * jnp.arange(L, dtype=F32) / L
    bands = jnp.linspace(1e-4, HY_BANDS - 1.0, HY_BANDS, dtype=F32)
    ph = w[:, None] * bands[None, :]
    z = jnp.concatenate([t[:, None], jnp.cos(ph), -jnp.sin(ph)], axis=-1)
    hp = lax.Precision.DEFAULT
    h = jnp.sin(freq[0] * (jnp.dot(z, w1, precision=hp) + b1))
    h = jnp.sin(freq[1] * (jnp.dot(h, w2, precision=hp) + b2))
    h = jnp.dot(h, w3, precision=hp).reshape(L, HY_ORDER, 2, BRANCH_W)
    deltas = jnp.abs(jnp.linspace(math.log(HY_TARGET) / HY_FAST, math.log(HY_TARGET) / HY_SLOW, BRANCH_W, dtype=F32))
    window = jnp.exp(-t[:, None] * deltas[None, :]) + HY_SHIFT
    h = h * window[:, None, None, :]
    kern = jnp.concatenate([h[:, :, 0], jnp.zeros((1, HY_ORDER, BRANCH_W), F32), jnp.flip(h[1:, :, 1], axis=0)], axis=0)
    kern = kern * lax.rsqrt(jnp.sum(kern * kern, axis=0, keepdims=True) + EPS)
    spec = jnp.fft.fft(kern, axis=0)
    return jnp.real(spec), jnp.imag(spec)


def _fft_consts(seq_len):
    n = 2 * seq_len
    n1 = n // FFT_N2
    n1h = seq_len // FFT_N2
    c1, s1 = _dft(n1, n1h, n1, -1.0)
    f1 = _const_split(c1) + _const_split(s1)
    c2, s2 = _dft(FFT_N2, FFT_N2, FFT_N2, -1.0)
    f2 = _const_split(c2) + _const_split(s2)
    cg, sg = _dft(n1h, n1, n1, 1.0)
    g1 = _const_split(cg) + _const_split(sg)
    k1 = np.arange(n1)[:, None].astype(np.float64)
    n2 = np.arange(FFT_N2)[None, :].astype(np.float64)
    ang = -2.0 * np.pi * (k1 * n2) / n
    twr = jnp.asarray(np.repeat(np.cos(ang)[:, :, None], LANES, axis=2), F32)
    twi = jnp.asarray(np.repeat(np.sin(ang)[:, :, None], LANES, axis=2), F32)
    return f1, f2, g1, twr, twi


def _dft_consts_direct(seq_len):
    n = 2 * seq_len
    c, s = _dft(n, seq_len, n, -1.0)
    fwd = _const_split(c) + _const_split(s)
    c, s = _dft(seq_len, n, n, 1.0)
    inv = _const_split(c) + _const_split(s)
    return fwd, inv


def _hyena(p, lp, *, batch, lat_len, ctx_len):
    assert batch == 2
    lat_rows = batch * lat_len
    outs = []
    for seq_len, row0 in ((lat_len, 0), (ctx_len, lat_rows)):
        z = _hy_short(p, lp['hy_short_w'], lp['hy_short_b'], row0=row0, seq_len=seq_len, batch=batch)
        spec_r, spec_i = _hyena_filter_spectrum(seq_len, lp['hy_w1'], lp['hy_b1'], lp['hy_w2'], lp['hy_b2'],
                                                lp['hy_w3'], lp['hy_freq'])
        y = z[0]
        for o in range(HY_ORDER):
            sr, si = spec_r[:, o], spec_i[:, o]
            if seq_len >= FFT_MIN_LEN:
                lay = lambda s: s.reshape(FFT_N2, 2 * seq_len // FFT_N2, BRANCH_W).transpose(1, 0, 2)
                y = _hy_longconv_lat(y, z[o + 1], lp['hy_bias'][o], lay(sr), lay(si), _fft_consts(seq_len),
                                     seq_len=seq_len)
            else:
                y = _hy_longconv_ctx(y, z[o + 1], lp['hy_bias'][o], sr, si, _dft_consts_direct(seq_len),
                                     seq_len=seq_len)
        outs.append(y.reshape(batch * seq_len, BRANCH_W))
    return jnp.concatenate(outs, axis=0)


def _merge_kernel(h_ref, mod_ref, ng_ref, y_ref, mw_ref, mb_ref, bw_ref, o_ref, xn_ref):
    c, br = pl.program_id(1), pl.program_id(2)

    @pl.when((br == 0) & (c == 0))
    def _():
        xn = _rms(h_ref[...], ng_ref[2:3, :]) * (1.0 + mod_ref[4:5, :]) + mod_ref[3:4, :]
        xn_ref[...] = xn.astype(BF16)

    g = jax.nn.sigmoid(_dot(xn_ref[...], mw_ref[...]) + mb_ref[...])
    term = g * _dot(y_ref[...].astype(BF16), bw_ref[...])

    @pl.when(br == 0)
    def _():
        o_ref[...] = term

    @pl.when(br != 0)
    def _():
        o_ref[...] += term


def _merge(h, mods, ng, ys, merge_w, merge_b, branch_w, *, lat_tiles, tiles_per_batch):
    r, d = h.shape
    tm, tn = ROW_TILE, 1024
    nc = d // tn
    midx = lambda i, br, c: (_mod_index(i, lat_tiles, tiles_per_batch), 0, 0)
    acc = pl.pallas_call(
        _merge_kernel,
        grid=(r // tm, nc, N_BRANCH),
        in_specs=[pl.BlockSpec((tm, d), lambda i, c, br: (i, 0)),
                  pl.BlockSpec((None, N_MOD, d), lambda i, c, br: midx(i, br, c)),
                  pl.BlockSpec((6, d), lambda i, c, br: (0, 0)),
                  pl.BlockSpec((None, tm, BRANCH_W), lambda i, c, br: (br, i, 0)),
                  pl.BlockSpec((d, tn), lambda i, c, br: (0, br * nc + c)),
                  pl.BlockSpec((1, tn), lambda i, c, br: (0, br * nc + c)),
                  pl.BlockSpec((None, BRANCH_W, tn), lambda i, c, br: (br, 0, c))],
        out_specs=pl.BlockSpec((tm, tn), lambda i, c, br: (i, c)),
        out_shape=jax.ShapeDtypeStruct((r, d), F32),
        scratch_shapes=[pltpu.VMEM((tm, d), BF16)],
        compiler_params=_cparams(("parallel", "arbitrary", "arbitrary")),
        name="merge",
    )(h, mods, ng, ys, merge_w, merge_b.reshape(1, -1), branch_w)
    return acc


def _outproj_kernel(a_ref, h_ref, mod_ref, ng_ref, w_ref, o_ref):
    y = _dot(a_ref[...].astype(BF16), w_ref[...])
    o_ref[...] = h_ref[...] + mod_ref[5:6, :] * _rms(y, ng_ref[3:4, :])


def _outproj(acc, h, mods, ng, w_out, *, lat_tiles, tiles_per_batch):
    r, d = h.shape
    tm = ROW_TILE
    midx = lambda i: (_mod_index(i, lat_tiles, tiles_per_batch), 0, 0)
    return pl.pallas_call(
        _outproj_kernel,
        grid=(r // tm,),
        in_specs=[pl.BlockSpec((tm, d), lambda i: (i, 0)),
                  pl.BlockSpec((tm, d), lambda i: (i, 0)),
                  pl.BlockSpec((None, N_MOD, d), midx),
                  pl.BlockSpec((6, d), lambda i: (0, 0)),
                  pl.BlockSpec((d, d), lambda i: (0, 0))],
        out_specs=pl.BlockSpec((tm, d), lambda i: (i, 0)),
        out_shape=jax.ShapeDtypeStruct((r, d), F32),
        compiler_params=_cparams(("parallel",)),
        name="outproj",
    )(acc, h, mods, ng, w_out)


def _rope_tables(batch, lat_len, ctx_len):
    rows = lat_len // GRID_W
    row = np.repeat(np.arange(rows, dtype=np.float32), GRID_W)
    col = np.tile(np.arange(GRID_W, dtype=np.float32), rows)
    n_freq = RET_DK // 4
    inv = (ROPE_BASE ** (-np.arange(n_freq, dtype=np.float32) / n_freq)).astype(np.float32)
    ang = np.concatenate([row[:, None] * inv, col[:, None] * inv], axis=-1).astype(np.float32)
    cos, sin = np.cos(ang), np.sin(ang)
    cos2 = np.concatenate([cos, cos], axis=-1)
    sin2 = np.concatenate([-sin, sin], axis=-1)
    ctx_rows = batch * ctx_len
    cos_t = np.concatenate([np.tile(cos2, (batch, 1)), np.ones((ctx_rows, RET_DK), np.float32)], axis=0)
    sin_t = np.concatenate([np.tile(sin2, (batch, 1)), np.zeros((ctx_rows, RET_DK), np.float32)], axis=0)
    return jnp.asarray(cos_t, F32), jnp.asarray(sin_t, F32)


def _reorder_w_in(w):
    pad = jnp.zeros((w.shape[0], P_COLS - w.shape[1]), w.dtype)
    lr0 = COL_HY
    lr1 = lr0 + 2 * GLA_RANK
    return jnp.concatenate([w[:, :lr0], w[:, lr1:], w[:, lr0:lr1], pad], axis=1)


def kernel(x, c, ctx, c_ctx, w_ada, b_ada, norm_gain, ffn_w_gate, ffn_w_up, ffn_w_down, w_in, ret_decay_logit, ret_norm_gain, s5_a_re, s5_a_im, s5_log_dt, s5_b_re, s5_b_im, s5_c_re, s5_c_im, s5_d, s5_glu_w, s5_glu_b, gla_gate_w, gla_gate_b, gla_norm_gain, hy_short_w, hy_short_b, hy_w1, hy_b1, hy_w2, hy_b2, hy_w3, hy_freq, hy_bias, branch_w, merge_w, merge_b, w_out):
    batch, lat_len, d = x.shape
    ctx_len = ctx.shape[1]
    depth = w_ada.shape[0]
    lat_rows = batch * lat_len
    tiles_per_batch = lat_len // ROW_TILE
    lat_tiles = batch * tiles_per_batch
    assert lat_len % ROW_TILE == 0 and (batch * ctx_len) % ROW_TILE == 0
    tile_kw = dict(lat_tiles=lat_tiles, tiles_per_batch=tiles_per_batch)
    seq_kw = dict(batch=batch, lat_len=lat_len, ctx_len=ctx_len)

    cond = jnp.zeros((8, d), F32).at[:batch].set(c).at[batch].set(c_ctx)
    mods = _ada(cond, w_ada, b_ada)[:, :batch + 1].reshape(depth, batch + 1, N_MOD, d)
    cos_t, sin_t = _rope_tables(batch, lat_len, ctx_len)

    h = jnp.concatenate([x.reshape(lat_rows, d), ctx.reshape(batch * ctx_len, d)], axis=0)
    params = dict(s5_a_re=s5_a_re, s5_a_im=s5_a_im, s5_log_dt=s5_log_dt, s5_b_re=s5_b_re, s5_b_im=s5_b_im,
                  s5_c_re=s5_c_re, s5_c_im=s5_c_im, s5_d=s5_d, s5_glu_w=s5_glu_w, s5_glu_b=s5_glu_b,
                  hy_short_w=hy_short_w, hy_short_b=hy_short_b, hy_w1=hy_w1, hy_b1=hy_b1, hy_w2=hy_w2,
                  hy_b2=hy_b2, hy_w3=hy_w3, hy_freq=hy_freq, hy_bias=hy_bias)
    for l in range(depth):
        lp = {k: v[l] for k, v in params.items()}
        ng = norm_gain[l]
        bf = lambda w: w.astype(BF16)
        h = _ffn(h, mods[l], ng, bf(ffn_w_gate[l, 0]), bf(ffn_w_up[l, 0]), bf(ffn_w_down[l, 0]),
                 mbase=0, gbase=0, **tile_kw)
        p = _inproj(h, mods[l], ng, bf(_reorder_w_in(w_in[l])), **tile_kw)
        y_ret = _retention(p, cos_t, sin_t, ret_decay_logit[l], ret_norm_gain[l], **seq_kw)
        y_s5 = _s5(p, lp, **seq_kw)
        y_gla = _gla(p, gla_gate_w[l], gla_gate_b[l], gla_norm_gain[l], **seq_kw)
        y_hy = _hyena(p, lp, **seq_kw)
        ys = jnp.stack([y_ret, y_s5, y_gla, y_hy], axis=0)
        acc = _merge(h, mods[l], ng, ys, bf(merge_w[l]), merge_b[l], bf(branch_w[l]), **tile_kw)
        h = _outproj(acc, h, mods[l], ng, bf(w_out[l]), **tile_kw)
        h = _ffn(h, mods[l], ng, bf(ffn_w_gate[l, 1]), bf(ffn_w_up[l, 1]), bf(ffn_w_down[l, 1]),
                 mbase=6, gbase=4, **tile_kw)
    return h[:lat_rows].reshape(batch, lat_len, d)
```

```python
import functools
import math

import jax
import jax.numpy as jnp
import numpy as np
from jax import lax
from jax.experimental import pallas as pl
from jax.experimental.pallas import tpu as pltpu

F32 = jnp.float32
BF16 = jnp.bfloat16

D_MODEL = 2048
N_MOD = 9
D_FF = 5632
EPS = 1e-6
GRID_W = 64
BRANCH_W = 1024
N_BRANCH = 4

RET_HEADS, RET_DK, RET_DV, RET_CHUNK = 4, 128, 256, 128
ROPE_BASE = 10000.0
S5_GROUP, S5_GROUPS, S5_STATE = 16, 64, 64
GLA_HEADS, GLA_DK, GLA_DV, GLA_RANK, GLA_TAU, GLA_CHUNK = 4, 128, 256, 16, 16.0, 64
GLA_SUB = 16
HY_ORDER, HY_SHORT, HY_BANDS, HY_HIDDEN = 2, 3, 16, 64
HY_TARGET, HY_FAST, HY_SLOW, HY_SHIFT = 1e-2, 0.3, 1.5, 0.05

LANES = 128
ROW_TILE = 512
VMEM_LIMIT = 56 * 1024 * 1024

P_COLS = 10368
COL_RET_Q, COL_RET_K, COL_RET_V, COL_RET_G = 0, 512, 1024, 2048
COL_S5 = 3072
COL_GLA_Q, COL_GLA_K, COL_GLA_V, COL_GLA_G = 4096, 4608, 5120, 6144
COL_HY = 7168
COL_LR = 10240

S5_T = 16
S5_BLK = LANES // S5_GROUP
S5_NBLK = S5_GROUPS // S5_BLK
S5_W = S5_BLK * S5_STATE
S5_LAT_TILE = 4096

FFT_N2 = 128
FFT_MIN_LEN = 2048
FFT_DATA_PASSES = 1


def _cparams(sem):
    return pltpu.CompilerParams(dimension_semantics=sem, vmem_limit_bytes=VMEM_LIMIT)


def _dot(a, b):
    return jnp.dot(a, b, preferred_element_type=F32)


def _dot_nt(a, b):
    return lax.dot_general(a, b, (((1,), (1,)), ((), ())), preferred_element_type=F32)


def _dot_tn(a, b):
    return lax.dot_general(a, b, (((0,), (0,)), ((), ())), preferred_element_type=F32)


def _split2(x):
    hi = x.astype(BF16)
    lo = (x - hi.astype(F32)).astype(BF16)
    return hi, lo


def _split3(x):
    hi = x.astype(BF16)
    r = x - hi.astype(F32)
    mid = r.astype(BF16)
    lo = (r - mid.astype(F32)).astype(BF16)
    return hi, mid, lo


def _operand(x, passes):
    return _split2(x) if passes == 3 else (x.astype(BF16), None)


def _const_operand(hi_ref, lo_ref, passes):
    return (hi_ref[...], lo_ref[...] if passes == 3 else None)


def _mm(a, b):
    out = _dot(a[0], b[0])
    if a[1] is not None and b[1] is not None:
        out = out + (_dot(a[0], b[1]) + _dot(a[1], b[0]))
    return out


def _cmm(ar, ai, br, bi):
    return _mm(ar, br) - _mm(ai, bi), _mm(ar, bi) + _mm(ai, br)


def _neg(a):
    return (-a[0], None if a[1] is None else -a[1])


def _rms(x, gain):
    return x * lax.rsqrt(jnp.mean(x * x, axis=-1, keepdims=True) + EPS) * gain


def _silu(x):
    return x * jax.nn.sigmoid(x)


def _log_sigmoid(x):
    return jnp.minimum(x, 0.0) - jnp.log1p(jnp.exp(-jnp.abs(x)))


def _mod_index(i, lat_tiles, tiles_per_batch):
    return jnp.where(i < lat_tiles, i // tiles_per_batch, lat_tiles // tiles_per_batch)


def _ada_kernel(c_ref, w_ref, b_ref, o_ref):
    cs = c_ref[...]
    o_ref[...] = _dot(_silu(cs).astype(BF16), w_ref[...].astype(BF16)) + b_ref[...]


def _ada(cond, w_ada, b_ada):
    depth, d, n = w_ada.shape
    tn = 1024
    return pl.pallas_call(
        _ada_kernel,
        grid=(depth, n // tn),
        in_specs=[pl.BlockSpec((8, d), lambda l, j: (0, 0)),
                  pl.BlockSpec((None, d, tn), lambda l, j: (l, 0, j)),
                  pl.BlockSpec((None, 1, tn), lambda l, j: (l, 0, j))],
        out_specs=pl.BlockSpec((None, 8, tn), lambda l, j: (l, 0, j)),
        out_shape=jax.ShapeDtypeStruct((depth, 8, n), F32),
        compiler_params=_cparams(("parallel", "parallel")),
        name="adaln",
    )(cond, w_ada, b_ada.reshape(depth, 1, n))


class _Tiles:
    def __init__(self, layer, lat_tiles, tiles_per_batch, n_tiles):
        self.layer, self.lat_tiles, self.tiles_per_batch, self.n_tiles = layer, lat_tiles, tiles_per_batch, n_tiles

    def mod_spec(self, d, nax):
        l, lt, tpb = self.layer, self.lat_tiles, self.tiles_per_batch
        return pl.BlockSpec((None, None, N_MOD, d), lambda i, *_: (l, _mod_index(i, lt, tpb), 0, 0))

    def gain_spec(self, d):
        l = self.layer
        return pl.BlockSpec((None, 6, d), lambda i, *_: (l, 0, 0))


def _ffn_kernel(h_ref, mod_ref, ng_ref, wg_ref, wu_ref, wd_ref, o_ref, xn_ref, acc_ref, *, mbase, gbase):
    j = pl.program_id(1)

    @pl.when(j == 0)
    def _():
        x = h_ref[...]
        xn = _rms(x, ng_ref[gbase:gbase + 1, :]) * (1.0 + mod_ref[mbase + 1:mbase + 2, :]) + mod_ref[mbase:mbase + 1, :]
        xn_ref[...] = xn.astype(BF16)
        acc_ref[...] = jnp.zeros_like(acc_ref)

    xn = xn_ref[...]
    a = _silu(_dot(xn, wg_ref[...])) * _dot(xn, wu_ref[...])
    acc_ref[...] += _dot(a.astype(BF16), wd_ref[...])

    @pl.when(j == pl.num_programs(1) - 1)
    def _():
        r = _rms(acc_ref[...], ng_ref[gbase + 1:gbase + 2, :])
        o_ref[...] = h_ref[...] + 0.5 * mod_ref[mbase + 2:mbase + 3, :] * r


def _ffn(h, mods, norm_gain, wg, wu, wd, tiles, *, idx):
    d = h.shape[1]
    f = wg.shape[-1]
    tm, tf = ROW_TILE, 512
    l = tiles.layer
    return pl.pallas_call(
        functools.partial(_ffn_kernel, mbase=6 * idx, gbase=4 * idx),
        grid=(tiles.n_tiles, f // tf),
        in_specs=[pl.BlockSpec((tm, d), lambda i, j: (i, 0)),
                  tiles.mod_spec(d, 2),
                  tiles.gain_spec(d),
                  pl.BlockSpec((None, None, d, tf), lambda i, j: (l, idx, 0, j)),
                  pl.BlockSpec((None, None, d, tf), lambda i, j: (l, idx, 0, j)),
                  pl.BlockSpec((None, None, tf, d), lambda i, j: (l, idx, j, 0))],
        out_specs=pl.BlockSpec((tm, d), lambda i, j: (i, 0)),
        out_shape=jax.ShapeDtypeStruct((tiles.n_tiles * tm, d), F32),
        scratch_shapes=[pltpu.VMEM((tm, d), BF16), pltpu.VMEM((tm, d), F32)],
        compiler_params=_cparams(("parallel", "arbitrary")),
        name="ffn",
    )(h, mods, norm_gain, wg, wu, wd)


def _inproj_kernel(h_ref, mod_ref, ng_ref, w_ref, o_ref, xn_ref):
    @pl.when(pl.program_id(1) == 0)
    def _():
        xn = _rms(h_ref[...], ng_ref[2:3, :]) * (1.0 + mod_ref[4:5, :]) + mod_ref[3:4, :]
        xn_ref[...] = xn.astype(BF16)

    o_ref[...] = _dot(xn_ref[...], w_ref[...])


def _inproj(h, mods, norm_gain, w, tiles):
    r, d = h.shape
    n = w.shape[-1]
    tm, tn = ROW_TILE, 1152
    l = tiles.layer
    return pl.pallas_call(
        _inproj_kernel,
        grid=(r // tm, n // tn),
        in_specs=[pl.BlockSpec((tm, d), lambda i, j: (i, 0)),
                  tiles.mod_spec(d, 2),
                  tiles.gain_spec(d),
                  pl.BlockSpec((None, d, tn), lambda i, j: (l, 0, j))],
        out_specs=pl.BlockSpec((tm, tn), lambda i, j: (i, j)),
        out_shape=jax.ShapeDtypeStruct((r, n), F32),
        scratch_shapes=[pltpu.VMEM((tm, d), BF16)],
        compiler_params=_cparams(("parallel", "arbitrary")),
        name="inproj",
    )(h, mods, norm_gain, w)


def _chunk_row_block(b, j, *, lat_chunks, ctx_chunks, reverse, batch):
    ctx0 = batch * lat_chunks + b * ctx_chunks
    lat0 = b * lat_chunks
    jl = j - ctx_chunks
    if reverse:
        return jnp.where(j < ctx_chunks, ctx0 + (ctx_chunks - 1 - j), lat0 + (lat_chunks - 1 - jl))
    return jnp.where(j < ctx_chunks, ctx0 + j, lat0 + jl)


def _ret_kernel(*refs, reverse):
    if reverse:
        dl_ref, q_ref, k_ref, v_ref, cos_ref, sin_ref, of_ref, g_ref, gain_ref, o_ref, s_ref = refs
    else:
        dl_ref, q_ref, k_ref, v_ref, cos_ref, sin_ref, o_ref, s_ref = refs
    C = RET_CHUNK

    @pl.when(pl.program_id(1) == 0)
    def _():
        s_ref[...] = jnp.zeros_like(s_ref)

    cos, sin = cos_ref[...], sin_ref[...]
    ti = lax.broadcasted_iota(jnp.int32, (C, C), 0).astype(F32)
    si = lax.broadcasted_iota(jnp.int32, (C, C), 1).astype(F32)
    tr = lax.broadcasted_iota(jnp.int32, (C, RET_DK), 0).astype(F32)
    rel = (si - ti) if reverse else (ti - si)
    scale = RET_DK ** -0.5
    for h in range(RET_HEADS):
        lg_cc = _log_sigmoid(jnp.full((C, C), dl_ref[h], F32))
        lg_cd = _log_sigmoid(jnp.full((C, RET_DK), dl_ref[h], F32))
        d_intra = jnp.where(rel >= 0, jnp.exp(lg_cc * jnp.maximum(rel, 0.0)), 0.0)
        if reverse:
            d_q = jnp.exp((C - tr) * lg_cd)
            d_k = jnp.exp(tr * lg_cd)
        else:
            d_q = jnp.exp((tr + 1.0) * lg_cd)
            d_k = jnp.exp((C - 1.0 - tr) * lg_cd)
        d_c = jnp.exp(C * lg_cd[0:1, 0:1])

        qh = q_ref[:, h * RET_DK:(h + 1) * RET_DK]
        kh = k_ref[:, h * RET_DK:(h + 1) * RET_DK]
        vh = v_ref[:, h * RET_DV:(h + 1) * RET_DV].astype(BF16)
        qr = qh * cos + pltpu.roll(qh, RET_DK // 2, axis=1) * sin
        kr = (kh * cos + pltpu.roll(kh, RET_DK // 2, axis=1) * sin) * scale

        scores = _dot_nt(qr.astype(BF16), kr.astype(BF16)) * d_intra
        s_old = s_ref[h]
        o = _dot(scores.astype(BF16), vh) + _dot((qr * d_q).astype(BF16), s_old.astype(BF16))
        s_ref[h] = s_old * d_c + _dot_tn((kr * d_k).astype(BF16), vh)

        if reverse:
            o = o + of_ref[:, h * RET_DV:(h + 1) * RET_DV]
            mu = jnp.mean(o, axis=-1, keepdims=True)
            oc = o - mu
            var = jnp.mean(oc * oc, axis=-1, keepdims=True)
            on = oc * lax.rsqrt(var + EPS) * gain_ref[:, h * RET_DV:(h + 1) * RET_DV]
            o = _silu(g_ref[:, h * RET_DV:(h + 1) * RET_DV]) * on
        o_ref[:, h * RET_DV:(h + 1) * RET_DV] = o.astype(o_ref.dtype)


def _retention(p, cos_t, sin_t, decay_logit, norm_gain, *, layer, batch, lat_len, ctx_len):
    r = p.shape[0]
    C = RET_CHUNK
    lat_chunks, ctx_chunks = lat_len // C, ctx_len // C
    steps = lat_chunks + ctx_chunks

    def call(reverse, extra):
        rb = functools.partial(_chunk_row_block, lat_chunks=lat_chunks, ctx_chunks=ctx_chunks,
                               reverse=reverse, batch=batch)
        row = lambda cb: (lambda b, j: (rb(b, j), cb))
        in_specs = [pl.BlockSpec(memory_space=pltpu.SMEM),
                    pl.BlockSpec((C, 512), row(COL_RET_Q // 512)),
                    pl.BlockSpec((C, 512), row(COL_RET_K // 512)),
                    pl.BlockSpec((C, 1024), row(COL_RET_V // 1024)),
                    pl.BlockSpec((C, RET_DK), row(0)),
                    pl.BlockSpec((C, RET_DK), row(0))]
        args = [decay_logit[layer, 1 if reverse else 0], p, p, p, cos_t, sin_t]
        if reverse:
            in_specs += [pl.BlockSpec((C, 1024), row(0)),
                         pl.BlockSpec((C, 1024), row(COL_RET_G // 1024)),
                         pl.BlockSpec((None, 1, 1024), lambda b, j: (layer, 0, 0))]
            args += [extra, p, norm_gain.reshape(-1, 1, 1024)]
        return pl.pallas_call(
            functools.partial(_ret_kernel, reverse=reverse),
            grid=(batch, steps),
            in_specs=in_specs,
            out_specs=pl.BlockSpec((C, 1024), row(0)),
            out_shape=jax.ShapeDtypeStruct((r, 1024), BF16 if reverse else F32),
            scratch_shapes=[pltpu.VMEM((RET_HEADS, RET_DK, RET_DV), F32)],
            compiler_params=_cparams(("parallel", "arbitrary")),
            name="retention_bwd" if reverse else "retention_fwd",
        )(*args)

    o_f = call(False, None)
    return call(True, o_f)


def _gla_kernel(*refs, reverse):
    if reverse:
        q_ref, k_ref, v_ref, a_ref, gw_ref, gb_ref, of_ref, g_ref, gain_ref, o_ref, s_ref = refs
    else:
        q_ref, k_ref, v_ref, a_ref, gw_ref, gb_ref, o_ref, s_ref = refs
    C, SB = GLA_CHUNK, GLA_SUB

    @pl.when(pl.program_id(1) == 0)
    def _():
        s_ref[...] = jnp.zeros_like(s_ref)

    la = _log_sigmoid(_dot(a_ref[...].astype(BF16), gw_ref[...]) + gb_ref[...]) / GLA_TAU
    la3 = _split3(la)
    ti = lax.broadcasted_iota(jnp.int32, (C, C), 0)
    si = lax.broadcasted_iota(jnp.int32, (C, C), 1)
    tri = jnp.where((si >= ti) if reverse else (si <= ti), 1.0, 0.0).astype(BF16)
    bcum = _dot(tri, la3[0]) + (_dot(tri, la3[1]) + _dot(tri, la3[2]))
    ones = jnp.ones((C, LANES), BF16)
    scale = GLA_DK ** -0.5
    end = 0 if reverse else C - 1

    for h in range(GLA_HEADS):
        sl = slice(h * GLA_DK, (h + 1) * GLA_DK)
        b = bcum[:, sl]
        q = q_ref[:, sl]
        k = k_ref[:, sl] * scale
        v = v_ref[:, h * GLA_DV:(h + 1) * GLA_DV].astype(BF16)
        s_old = s_ref[h]
        b_end = b[end:end + 1, :]
        o_inter = _dot((q * jnp.exp(b)).astype(BF16), s_old.astype(BF16))
        col = _dot_tn(la3[0][:, sl], ones) + (_dot_tn(la3[1][:, sl], ones) + _dot_tn(la3[2][:, sl], ones))
        dec_col = jnp.exp(jnp.concatenate([col, col], axis=1))
        s_ref[h] = s_old * dec_col + _dot_tn((k * jnp.exp(b_end - b)).astype(BF16), v)

        srow = lax.broadcasted_iota(jnp.int32, (C, GLA_DK), 0)
        for blk in range(C // SB):
            r0 = blk * SB
            ref_row = b[r0 + SB - 1:r0 + SB, :] if reverse else b[r0:r0 + 1, :]
            reach = (srow >= r0) if reverse else (srow < r0 + SB)
            qt = (q[r0:r0 + SB] * jnp.exp(b[r0:r0 + SB] - ref_row)).astype(BF16)
            kt = (k * jnp.exp(jnp.where(reach, ref_row - b, 0.0))).astype(BF16)
            sc = _dot_nt(qt, kt)
            tt = lax.broadcasted_iota(jnp.int32, sc.shape, 0) + r0
            ss = lax.broadcasted_iota(jnp.int32, sc.shape, 1)
            sc = jnp.where((ss >= tt) if reverse else (ss <= tt), sc, 0.0)
            o = _dot(sc.astype(BF16), v) + o_inter[r0:r0 + SB]
            cs = slice(h * GLA_DV, (h + 1) * GLA_DV)
            if reverse:
                o = o + of_ref[r0:r0 + SB, cs]
                on = o * lax.rsqrt(jnp.mean(o * o, axis=-1, keepdims=True) + EPS) * gain_ref[:, cs]
                o = _silu(g_ref[r0:r0 + SB, cs]) * on
            o_ref[r0:r0 + SB, cs] = o.astype(o_ref.dtype)


def _gla(p, gate_w, gate_b, norm_gain, *, layer, batch, lat_len, ctx_len):
    r = p.shape[0]
    C = GLA_CHUNK
    lat_chunks, ctx_chunks = lat_len // C, ctx_len // C
    steps = lat_chunks + ctx_chunks
    hk = GLA_HEADS * GLA_DK
    depth = gate_w.shape[0]
    gw = jnp.zeros((depth, 2, LANES, hk), F32)
    for d in range(2):
        gw = gw.at[:, d, d * GLA_RANK:(d + 1) * GLA_RANK].set(gate_w[:, d])
    gw = gw.astype(BF16)

    def call(reverse, extra):
        d = 1 if reverse else 0
        rb = functools.partial(_chunk_row_block, lat_chunks=lat_chunks, ctx_chunks=ctx_chunks,
                               reverse=reverse, batch=batch)
        row = lambda cb: (lambda b, j: (rb(b, j), cb))
        in_specs = [pl.BlockSpec((C, 512), row(COL_GLA_Q // 512)),
                    pl.BlockSpec((C, 512), row(COL_GLA_K // 512)),
                    pl.BlockSpec((C, 1024), row(COL_GLA_V // 1024)),
                    pl.BlockSpec((C, LANES), row(COL_LR // LANES)),
                    pl.BlockSpec((None, None, LANES, hk), lambda b, j: (layer, d, 0, 0)),
                    pl.BlockSpec((None, None, 1, hk), lambda b, j: (layer, d, 0, 0))]
        args = [p, p, p, p, gw, gate_b.reshape(depth, 2, 1, hk)]
        if reverse:
            in_specs += [pl.BlockSpec((C, 1024), row(0)),
                         pl.BlockSpec((C, 1024), row(COL_GLA_G // 1024)),
                         pl.BlockSpec((None, 1, 1024), lambda b, j: (layer, 0, 0))]
            args += [extra, p, norm_gain.reshape(depth, 1, 1024)]
        return pl.pallas_call(
            functools.partial(_gla_kernel, reverse=reverse),
            grid=(batch, steps),
            in_specs=in_specs,
            out_specs=pl.BlockSpec((C, 1024), row(0)),
            out_shape=jax.ShapeDtypeStruct((r, 1024), BF16 if reverse else F32),
            scratch_shapes=[pltpu.VMEM((GLA_HEADS, GLA_DK, GLA_DV), F32)],
            compiler_params=_cparams(("parallel", "arbitrary")),
            name="gla_bwd" if reverse else "gla_fwd",
        )(*args)

    o_f = call(False, None)
    return call(True, o_f)


def _s5_tables(a_re, a_im, log_dt, b_re, b_im, c_re, c_im):
    T, P, K = S5_T, S5_STATE, S5_GROUP
    lead = a_re.shape[:-2]
    nl = len(lead)
    dt = jnp.exp(log_dt)[..., None]
    mag = jnp.exp(a_re * dt)
    ab_re, ab_im = mag * jnp.cos(a_im * dt), mag * jnp.sin(a_im * dt)
    den = a_re * a_re + a_im * a_im
    nr = ab_re - 1.0
    f_re = (nr * a_re + ab_im * a_im) / den
    f_im = (ab_im * a_re - nr * a_im) / den
    bb_re = f_re[..., None] * b_re - f_im[..., None] * b_im
    bb_im = f_re[..., None] * b_im + f_im[..., None] * b_re

    def power(n):
        n = jnp.asarray(n, F32)[:, None, None]
        m = jnp.exp(n * (a_re * dt)[..., None, :, :])
        ang = n * (a_im * dt)[..., None, :, :]
        return m * jnp.cos(ang), m * jnp.sin(ang)

    eye = jnp.eye(S5_BLK, dtype=F32)

    def blockdiag_in(x):
        x = x.reshape(*lead, T, S5_NBLK, S5_BLK, P, K)
        x = jnp.moveaxis(x, nl, nl + 1)
        x = jnp.swapaxes(x, -1, -2)
        y = x[..., :, :, None, :] * eye[:, None, :, None]
        return y.reshape(*lead, S5_NBLK, T, LANES, S5_W)

    pr, pi = power(np.arange(T - 1, -1, -1))
    bre, bim = bb_re[..., None, :, :, :], bb_im[..., None, :, :, :]
    w_re = pr[..., None] * bre - pi[..., None] * bim
    w_im = pr[..., None] * bim + pi[..., None] * bre
    w_end = jnp.concatenate([blockdiag_in(w_re), blockdiag_in(w_im)], axis=-1).astype(BF16)

    qr, qi = power(np.arange(T))
    lb_re = qr[..., None] * bre - qi[..., None] * bim
    lb_im = qr[..., None] * bim + qi[..., None] * bre
    kj = (jnp.einsum('...gop,...jgpi->...jgio', c_re, lb_re)
          - jnp.einsum('...gop,...jgpi->...jgio', c_im, lb_im))
    kj = kj.reshape(*lead, T, S5_NBLK, S5_BLK, K, K)
    kj = jnp.moveaxis(kj, nl, nl + 1)
    kj = kj[..., :, :, None, :] * eye[:, None, :, None]
    kj = kj.reshape(*lead, S5_NBLK, T, LANES, LANES).astype(BF16)

    def blockdiag_out(x):
        x = jnp.swapaxes(x.reshape(*lead, S5_NBLK, S5_BLK, K, P), -1, -2)
        y = x[..., :, :, None, :] * eye[:, None, :, None]
        return y.reshape(*lead, S5_NBLK, S5_W, LANES)

    c_out = jnp.concatenate([blockdiag_out(c_re), -blockdiag_out(c_im)], axis=-2).astype(BF16)

    sr, si = power(np.arange(1, T + 1))
    lam_pow = jnp.concatenate([sr.reshape(*lead, T, S5_NBLK, 1, S5_W), si.reshape(*lead, T, S5_NBLK, 1, S5_W)], axis=-1)
    lam_pow = jnp.moveaxis(lam_pow, nl, nl + 1)
    return w_end, kj, c_out, lam_pow


def _s5_kernel(u_ref, x0_ref, wend_ref, kj_ref, cout_ref, lam_ref, y_ref, xf_ref,
               ubuf_ref, sloc_ref, xin_ref, x_ref, *, reverse, n):
    T, W = S5_T, S5_W

    @pl.when(pl.program_id(2) == 0)
    def _():
        x_ref[...] = x0_ref[...]

    off = lambda r: (T - 1 - r) if reverse else r
    for r in range(T):
        ubuf_ref[r] = u_ref[pl.ds(off(r), n, stride=T), :].astype(BF16)

    sloc = _dot(ubuf_ref[0], wend_ref[0])
    for r in range(1, T):
        sloc = sloc + _dot(ubuf_ref[r], wend_ref[r])
    sloc_ref[...] = sloc

    lt = lam_ref[T - 1]
    ltr, lti = lt[:, :W], lt[:, W:]

    def body(i, x):
        c = (n - 1 - i) if reverse else i
        xin_ref[pl.ds(c, 1), :] = x
        s = sloc_ref[pl.ds(c, 1), :]
        xr, xi = x[:, :W], x[:, W:]
        return jnp.concatenate([ltr * xr - lti * xi + s[:, :W], ltr * xi + lti * xr + s[:, W:]], axis=1)

    x_fin = lax.fori_loop(0, n, body, x_ref[...])
    x_ref[...] = x_fin
    xf_ref[...] = x_fin

    xin = xin_ref[...]
    xr, xi = xin[:, :W], xin[:, W:]
    cout = cout_ref[...]
    for r in range(T):
        lp = lam_ref[r]
        pr, pi = lp[:, :W], lp[:, W:]
        z = jnp.concatenate([pr * xr - pi * xi, pr * xi + pi * xr], axis=1)
        y = _dot(z.astype(BF16), cout)
        for j in range(r + 1):
            y = y + _dot(ubuf_ref[r - j], kj_ref[j])
        y_ref[pl.ds(off(r), n, stride=T), :] = y


def _s5_scan(p, tables, x0, *, layer, reverse, row0, rows_per_batch, tile, batch):
    w_end, kj, c_out, lam_pow = tables
    T, W = S5_T, S5_W
    n = tile // T
    steps = rows_per_batch // tile
    blk0 = row0 // tile
    d = 1 if reverse else 0

    def urow(kb, b, j):
        jj = (steps - 1 - j) if reverse else j
        return blk0 + b * steps + jj

    tab = lambda *tail: pl.BlockSpec((None, None, None) + tail, lambda kb, b, j: (layer, d, kb) + (0,) * len(tail))
    y, xf = pl.pallas_call(
        functools.partial(_s5_kernel, reverse=reverse, n=n),
        grid=(S5_NBLK, batch, steps),
        in_specs=[pl.BlockSpec((tile, LANES), lambda kb, b, j: (urow(kb, b, j), COL_S5 // LANES + kb)),
                  pl.BlockSpec((None, 1, 2 * W), lambda kb, b, j: (b * S5_NBLK + kb, 0, 0)),
                  tab(T, LANES, 2 * W), tab(T, LANES, LANES), tab(2 * W, LANES), tab(T, 1, 2 * W)],
        out_specs=[pl.BlockSpec((tile, LANES), lambda kb, b, j: (urow(kb, b, j) - blk0, kb)),
                   pl.BlockSpec((None, 1, 2 * W), lambda kb, b, j: (b * S5_NBLK + kb, 0, 0))],
        out_shape=[jax.ShapeDtypeStruct((batch * rows_per_batch, BRANCH_W), F32),
                   jax.ShapeDtypeStruct((batch * S5_NBLK, 1, 2 * W), F32)],
        scratch_shapes=[pltpu.VMEM((T, n, LANES), BF16), pltpu.VMEM((n, 2 * W), F32),
                        pltpu.VMEM((n, 2 * W), F32), pltpu.VMEM((1, 2 * W), F32)],
        compiler_params=_cparams(("parallel", "parallel", "arbitrary")),
        name="s5_scan",
    )(p, x0, w_end, kj, c_out, lam_pow)
    return y, xf


def _gelu_tanh(x):
    return 0.5 * x * (1.0 + jnp.tanh(math.sqrt(2.0 / math.pi) * (x + 0.044715 * (x * x * x))))


def _s5_out_kernel(ylf_ref, ylb_ref, ycf_ref, ycb_ref, u_ref, d_ref, w_ref, b_ref, o_ref, *, lat_tiles):
    def emit(yf_ref, yb_ref):
        y = _gelu_tanh(yf_ref[...] + yb_ref[...] + d_ref[...] * u_ref[...])
        o_ref[...] = (y * jax.nn.sigmoid(_dot(y.astype(BF16), w_ref[...]) + b_ref[...])).astype(o_ref.dtype)

    i = pl.program_id(0)
    pl.when(i < lat_tiles)(lambda: emit(ylf_ref, ylb_ref))
    pl.when(i >= lat_tiles)(lambda: emit(ycf_ref, ycb_ref))


def _s5(p, tables, s5_d, glu_w, glu_b, *, layer, batch, lat_len, ctx_len):
    r = p.shape[0]
    lat_rows = batch * lat_len
    zeros = jnp.zeros((batch * S5_NBLK, 1, 2 * S5_W), F32)
    ys = []
    for d in range(2):
        kw = dict(layer=layer, reverse=bool(d), batch=batch)
        y_c, x_c = _s5_scan(p, tables, zeros, row0=lat_rows, rows_per_batch=ctx_len, tile=ctx_len, **kw)
        y_l, _ = _s5_scan(p, tables, x_c, row0=0, rows_per_batch=lat_len, tile=min(lat_len, S5_LAT_TILE), **kw)
        ys += [y_l, y_c]
    tm = ROW_TILE
    lat_tiles = lat_rows // tm
    depth = s5_d.shape[0]
    lat = pl.BlockSpec((tm, BRANCH_W), lambda i: (jnp.minimum(i, lat_tiles - 1), 0))
    ctx = pl.BlockSpec((tm, BRANCH_W), lambda i: (jnp.maximum(i - lat_tiles, 0), 0))
    return pl.pallas_call(
        functools.partial(_s5_out_kernel, lat_tiles=lat_tiles),
        grid=(r // tm,),
        in_specs=[lat, lat, ctx, ctx,
                  pl.BlockSpec((tm, BRANCH_W), lambda i: (i, COL_S5 // BRANCH_W)),
                  pl.BlockSpec((None, 1, BRANCH_W), lambda i: (layer, 0, 0)),
                  pl.BlockSpec((None, BRANCH_W, BRANCH_W), lambda i: (layer, 0, 0)),
                  pl.BlockSpec((None, 1, BRANCH_W), lambda i: (layer, 0, 0))],
        out_specs=pl.BlockSpec((tm, BRANCH_W), lambda i: (i, 0)),
        out_shape=jax.ShapeDtypeStruct((r, BRANCH_W), BF16),
        compiler_params=_cparams(("parallel",)),
        name="s5_out",
    )(ys[0], ys[2], ys[1], ys[3], p, s5_d.reshape(depth, 1, -1), glu_w, glu_b.reshape(depth, 1, -1))


def _hy_short_kernel(z_ref, w_ref, b_ref, o_ref):
    z = z_ref[...]
    n = z.shape[0]
    t = lax.broadcasted_iota(jnp.int32, z.shape, 0)
    zm = jnp.where(t == 0, 0.0, pltpu.roll(z, 1, axis=0))
    zp = jnp.where(t == n - 1, 0.0, pltpu.roll(z, n - 1, axis=0))
    o_ref[...] = w_ref[0:1, :] * zm + w_ref[1:2, :] * z + w_ref[2:3, :] * zp + b_ref[...]


def _hy_short(p, w, b, *, layer, row0, seq_len, batch):
    cb = 256
    nb = BRANCH_W // cb
    blk0 = row0 // seq_len
    depth = w.shape[0]
    return pl.pallas_call(
        _hy_short_kernel,
        grid=(HY_ORDER + 1, batch, nb),
        in_specs=[pl.BlockSpec((seq_len, cb), lambda s, bb, c: (blk0 + bb, COL_HY // cb + s * nb + c)),
                  pl.BlockSpec((None, HY_SHORT, cb), lambda s, bb, c: (layer, 0, s * nb + c)),
                  pl.BlockSpec((None, 1, cb), lambda s, bb, c: (layer, 0, s * nb + c))],
        out_specs=pl.BlockSpec((None, None, seq_len, cb), lambda s, bb, c: (s, bb, 0, c)),
        out_shape=jax.ShapeDtypeStruct((HY_ORDER + 1, batch, seq_len, BRANCH_W), F32),
        compiler_params=_cparams(("parallel", "parallel", "parallel")),
        name="hyena_short",
    )(p, w, b.reshape(depth, 1, -1))


def _dft(n_out, n_in, n, sign):
    k = np.arange(n_out)[:, None].astype(np.float64)
    m = np.arange(n_in)[None, :].astype(np.float64)
    ang = sign * 2.0 * np.pi * ((k * m) % n) / n
    return np.cos(ang), np.sin(ang)


def _const_split(x):
    return _split2(jnp.asarray(x, F32))


def _const_spec(shape):
    return pl.BlockSpec(shape, lambda *_: (0,) * len(shape))


def _fft_consts(seq_len):
    n = 2 * seq_len
    n1 = n // FFT_N2
    n1h = seq_len // FFT_N2
    c1, s1 = _dft(n1, n1, n1, -1.0)
    f1_full = _const_split(c1) + _const_split(s1)
    f1 = _const_split(c1[:, :n1h]) + _const_split(s1[:, :n1h])
    c2, s2 = _dft(FFT_N2, FFT_N2, FFT_N2, -1.0)
    f2 = _const_split(c2) + _const_split(s2)
    cg, sg = _dft(n1h, n1, n1, 1.0)
    g1 = _const_split(cg) + _const_split(sg)
    k1 = np.arange(n1)[:, None].astype(np.float64)
    n2 = np.arange(FFT_N2)[None, :].astype(np.float64)
    ang = -2.0 * np.pi * (k1 * n2) / n
    twr = jnp.asarray(np.repeat(np.cos(ang)[:, :, None], LANES, axis=2), F32)
    twi = jnp.asarray(np.repeat(np.sin(ang)[:, :, None], LANES, axis=2), F32)
    return dict(f1_full=f1_full, f1=f1, f2=f2, g1=g1, twr=twr, twi=twi)


def _dft_consts_direct(seq_len):
    n = 2 * seq_len
    c, s = _dft(n, n, n, -1.0)
    full = _const_split(c) + _const_split(s)
    fwd = _const_split(c[:, :seq_len]) + _const_split(s[:, :seq_len])
    c, s = _dft(seq_len, n, n, 1.0)
    inv = _const_split(c) + _const_split(s)
    return dict(full=full, fwd=fwd, inv=inv)


def _hyena_filter(L, w1, b1, w2, b2, w3, freq):
    pos = np.concatenate([np.arange(L), [0], np.arange(L - 1, 0, -1)])
    fwd = jnp.asarray((np.arange(2 * L) < L)[:, None, None], F32)
    bwd = jnp.asarray((np.arange(2 * L) > L)[:, None, None], F32)
    t = jnp.linspace(0.0, 1.0, L, dtype=F32)[pos]
    w = (2.0 * math.pi * jnp.arange(L, dtype=F32) / L)[pos]
    bands = jnp.linspace(1e-4, HY_BANDS - 1.0, HY_BANDS, dtype=F32)
    ph = w[:, None] * bands[None, :]
    z = jnp.concatenate([t[:, None], jnp.cos(ph), -jnp.sin(ph)], axis=-1)
    h = jnp.sin(freq[0] * (z @ w1 + b1))
    h = jnp.sin(freq[1] * (h @ w2 + b2))
    h = (h @ w3).reshape(2 * L, HY_ORDER, 2, BRANCH_W)
    deltas = jnp.abs(jnp.linspace(math.log(HY_TARGET) / HY_FAST, math.log(HY_TARGET) / HY_SLOW, BRANCH_W, dtype=F32))
    window = jnp.exp(-t[:, None] * deltas[None, :]) + HY_SHIFT
    kern = (h[:, :, 0] * fwd + h[:, :, 1] * bwd) * window[:, None, :]
    scale = lax.rsqrt(jnp.sum(kern * kern, axis=0) + EPS)
    return kern.reshape(2 * L, HY_ORDER * BRANCH_W), scale.reshape(HY_ORDER, 1, BRANCH_W)


def _spec_fwd1_kernel(z_ref, frh_ref, frl_ref, fih_ref, fil_ref, ar_ref, ai_ref):
    z = _split2(z_ref[...])
    ar_ref[...] = _mm((frh_ref[...], frl_ref[...]), z)
    ai_ref[...] = _mm((fih_ref[...], fil_ref[...]), z)


def _spec_fwd2_kernel(ar_ref, ai_ref, twr_ref, twi_ref, frh_ref, frl_ref, fih_ref, fil_ref, sc_ref,
                      hr_ref, hi_ref, *, reps):
    twr = jnp.concatenate([twr_ref[...]] * reps, axis=1)
    twi = jnp.concatenate([twi_ref[...]] * reps, axis=1)
    ar, ai = ar_ref[...], ai_ref[...]
    xr, xi = ar * twr - ai * twi, ar * twi + ai * twr
    sr, si = _cmm((frh_ref[...], frl_ref[...]), (fih_ref[...], fil_ref[...]), _split2(xr), _split2(xi))
    hr_ref[...] = sr * sc_ref[...]
    hi_ref[...] = si * sc_ref[...]


def _spec_direct_kernel(z_ref, frh_ref, frl_ref, fih_ref, fil_ref, sc_ref, hr_ref, hi_ref):
    z = _split2(z_ref[...])
    hr_ref[...] = _mm((frh_ref[...], frl_ref[...]), z) * sc_ref[...]
    hi_ref[...] = _mm((fih_ref[...], fil_ref[...]), z) * sc_ref[...]


def _hyena_spectrum(kern, scale, consts, *, seq_len):
    C = BRANCH_W
    n = 2 * seq_len
    if seq_len < FFT_MIN_LEN:
        cb = 256
        nb = C // cb
        return pl.pallas_call(
            _spec_direct_kernel,
            grid=(HY_ORDER, nb),
            in_specs=[pl.BlockSpec((n, cb), lambda o, i: (0, o * nb + i))] + [_const_spec((n, n))] * 4
                     + [pl.BlockSpec((None, 1, cb), lambda o, i: (o, 0, i))],
            out_specs=[pl.BlockSpec((None, n, cb), lambda o, i: (o, 0, i))] * 2,
            out_shape=[jax.ShapeDtypeStruct((HY_ORDER, n, C), F32)] * 2,
            compiler_params=_cparams(("parallel", "parallel")),
            name="hyena_spec_direct",
        )(kern, *consts['full'], scale)
    n1 = n // FFT_N2
    cols = FFT_N2 * C
    ar, ai = pl.pallas_call(
        _spec_fwd1_kernel,
        grid=(HY_ORDER, FFT_N2),
        in_specs=[pl.BlockSpec((n1, C), lambda o, i: (0, i * HY_ORDER + o))] + [_const_spec((n1, n1))] * 4,
        out_specs=[pl.BlockSpec((None, n1, C), lambda o, i: (o, 0, i))] * 2,
        out_shape=[jax.ShapeDtypeStruct((HY_ORDER, n1, cols), F32)] * 2,
        compiler_params=_cparams(("parallel", "parallel")),
        name="hyena_spec_fwd1",
    )(kern.reshape(n1, FFT_N2 * HY_ORDER * C), *consts['f1_full'])
    blk = pl.BlockSpec((None, None, FFT_N2, C), lambda o, i: (o, i, 0, 0))
    tw = pl.BlockSpec((None, FFT_N2, LANES), lambda o, i: (i, 0, 0))
    a4 = lambda a: a.reshape(HY_ORDER, n1, FFT_N2, C)
    return pl.pallas_call(
        functools.partial(_spec_fwd2_kernel, reps=C // LANES),
        grid=(HY_ORDER, n1),
        in_specs=[blk, blk, tw, tw] + [_const_spec((FFT_N2, FFT_N2))] * 4
                 + [pl.BlockSpec((None, 1, C), lambda o, i: (o, 0, 0))],
        out_specs=[blk, blk],
        out_shape=[jax.ShapeDtypeStruct((HY_ORDER, n1, FFT_N2, C), F32)] * 2,
        compiler_params=_cparams(("parallel", "parallel")),
        name="hyena_spec_fwd2",
    )(a4(ar), a4(ai), consts['twr'], consts['twi'], *consts['f2'], scale)


def _fft_fwd1_kernel(z_ref, frh_ref, frl_ref, fih_ref, fil_ref, ar_ref, ai_ref, *, passes):
    zr, zi = _operand(z_ref[0], passes), _operand(z_ref[1], passes)
    fr, fi = _const_operand(frh_ref, frl_ref, passes), _const_operand(fih_ref, fil_ref, passes)
    ar, ai = _cmm(fr, fi, zr, zi)
    ar_ref[...] = ar
    ai_ref[...] = ai


def _fft_mid_kernel(ar_ref, ai_ref, twr_ref, twi_ref, hr_ref, hi_ref, frh_ref, frl_ref, fih_ref, fil_ref,
                    br_ref, bi_ref, *, reps, passes):
    twr = jnp.concatenate([twr_ref[...]] * reps, axis=1)
    twi = jnp.concatenate([twi_ref[...]] * reps, axis=1)
    ar, ai = ar_ref[...], ai_ref[...]
    xr, xi = ar * twr - ai * twi, ar * twi + ai * twr
    fr, fi = _const_operand(frh_ref, frl_ref, passes), _const_operand(fih_ref, fil_ref, passes)
    sr, si = _cmm(fr, fi, _operand(xr, passes), _operand(xi, passes))
    hr, hi = hr_ref[...], hi_ref[...]
    yr, yi = sr * hr - si * hi, sr * hi + si * hr
    br, bi = _cmm(fr, _neg(fi), _operand(yr, passes), _operand(yi, passes))
    br_ref[...] = br * twr + bi * twi
    bi_ref[...] = bi * twr - br * twi


def _fft_inv1_kernel(br_ref, bi_ref, grh_ref, grl_ref, gih_ref, gil_ref, gate_ref, y_ref, bias_ref, o_ref, *,
                     inv_n, passes):
    gr, gi = _const_operand(grh_ref, grl_ref, passes), _const_operand(gih_ref, gil_ref, passes)
    outr, outi = _cmm(gr, gi, _operand(br_ref[...], passes), _operand(bi_ref[...], passes))
    bias = bias_ref[...]
    o_ref[0] = (gate_ref[0] * (outr * inv_n + y_ref[0] * bias)).astype(o_ref.dtype)
    o_ref[1] = (gate_ref[1] * (outi * inv_n + y_ref[1] * bias)).astype(o_ref.dtype)


def _hy_longconv_lat(y, gate, bias, spec, consts, *, layer, order, seq_len, out_dtype):
    n1 = 2 * seq_len // FFT_N2
    n1h = seq_len // FFT_N2
    C = y.shape[-1]
    cols = FFT_N2 * C
    passes = FFT_DATA_PASSES
    yv = y.reshape(2, n1h, cols)
    cw = C
    ar, ai = pl.pallas_call(
        functools.partial(_fft_fwd1_kernel, passes=passes),
        grid=(cols // cw,),
        in_specs=[pl.BlockSpec((2, n1h, cw), lambda i: (0, 0, i))] + [_const_spec((n1, n1h))] * 4,
        out_specs=[pl.BlockSpec((n1, cw), lambda i: (0, i))] * 2,
        out_shape=[jax.ShapeDtypeStruct((n1, cols), F32)] * 2,
        compiler_params=_cparams(("parallel",)),
        name="hyena_fft_fwd1",
    )(yv, *consts['f1'])
    a3 = lambda a: a.reshape(n1, FFT_N2, C)
    blk = pl.BlockSpec((None, FFT_N2, C), lambda i: (i, 0, 0))
    sblk = pl.BlockSpec((None, None, FFT_N2, C), lambda i: (order, i, 0, 0))
    tw = pl.BlockSpec((None, FFT_N2, LANES), lambda i: (i, 0, 0))
    br, bi = pl.pallas_call(
        functools.partial(_fft_mid_kernel, reps=C // LANES, passes=passes),
        grid=(n1,),
        in_specs=[blk, blk, tw, tw, sblk, sblk] + [_const_spec((FFT_N2, FFT_N2))] * 4,
        out_specs=[blk, blk],
        out_shape=[jax.ShapeDtypeStruct((n1, FFT_N2, C), F32)] * 2,
        compiler_params=_cparams(("parallel",)),
        name="hyena_fft_mid",
    )(a3(ar), a3(ai), consts['twr'], consts['twi'], spec[0], spec[1], *consts['f2'])
    depth = bias.shape[0]
    out = pl.pallas_call(
        functools.partial(_fft_inv1_kernel, inv_n=1.0 / (n1 * FFT_N2), passes=passes),
        grid=(cols // cw,),
        in_specs=[pl.BlockSpec((n1, cw), lambda i: (0, i))] * 2 + [_const_spec((n1h, n1))] * 4
                 + [pl.BlockSpec((2, n1h, cw), lambda i: (0, 0, i))] * 2
                 + [pl.BlockSpec((None, None, 1, cw), lambda i: (layer, order, 0, 0))],
        out_specs=pl.BlockSpec((2, n1h, cw), lambda i: (0, 0, i)),
        out_shape=jax.ShapeDtypeStruct((2, n1h, cols), out_dtype),
        compiler_params=_cparams(("parallel",)),
        name="hyena_fft_inv1",
    )(br.reshape(n1, cols), bi.reshape(n1, cols), *consts['g1'], gate.reshape(2, n1h, cols), yv,
      bias.reshape(depth, HY_ORDER, 1, C))
    return out.reshape(2, seq_len, C)


def _hy_direct_kernel(y_ref, gate_ref, bias_ref, hr_ref, hi_ref, frh_ref, frl_ref, fih_ref, fil_ref,
                      grh_ref, grl_ref, gih_ref, gil_ref, o_ref, *, inv_n, passes):
    fr, fi = _const_operand(frh_ref, frl_ref, passes), _const_operand(fih_ref, fil_ref, passes)
    sr, si = _cmm(fr, fi, _operand(y_ref[0], passes), _operand(y_ref[1], passes))
    hr, hi = hr_ref[...], hi_ref[...]
    yr, yi = sr * hr - si * hi, sr * hi + si * hr
    gr, gi = _const_operand(grh_ref, grl_ref, passes), _const_operand(gih_ref, gil_ref, passes)
    outr, outi = _cmm(gr, gi, _operand(yr, passes), _operand(yi, passes))
    bias = bias_ref[...]
    o_ref[0] = (gate_ref[0] * (outr * inv_n + y_ref[0] * bias)).astype(o_ref.dtype)
    o_ref[1] = (gate_ref[1] * (outi * inv_n + y_ref[1] * bias)).astype(o_ref.dtype)


def _hy_longconv_direct(y, gate, bias, spec, consts, *, layer, order, seq_len, out_dtype):
    C = y.shape[-1]
    cb = 256
    n = 2 * seq_len
    depth = bias.shape[0]
    data = pl.BlockSpec((2, seq_len, cb), lambda i: (0, 0, i))
    sblk = pl.BlockSpec((None, n, cb), lambda i: (order, 0, i))
    return pl.pallas_call(
        functools.partial(_hy_direct_kernel, inv_n=1.0 / n, passes=FFT_DATA_PASSES),
        grid=(C // cb,),
        in_specs=[data, data, pl.BlockSpec((None, None, 1, cb), lambda i: (layer, order, 0, i)), sblk, sblk]
                 + [_const_spec((n, seq_len))] * 4 + [_const_spec((seq_len, n))] * 4,
        out_specs=data,
        out_shape=jax.ShapeDtypeStruct((2, seq_len, C), out_dtype),
        compiler_params=_cparams(("parallel",)),
        name="hyena_direct_conv",
    )(y, gate, bias.reshape(depth, HY_ORDER, 1, C), spec[0], spec[1], *consts['fwd'], *consts['inv'])


def _hyena(p, filt, hy_short_w, hy_short_b, hy_bias, *, layer, batch, lat_len, ctx_len):
    assert batch == 2
    lat_rows = batch * lat_len
    outs = []
    for seq_len, row0 in ((lat_len, 0), (ctx_len, lat_rows)):
        z = _hy_short(p, hy_short_w, hy_short_b, layer=layer, row0=row0, seq_len=seq_len, batch=batch)
        fft = seq_len >= FFT_MIN_LEN
        consts = _fft_consts(seq_len) if fft else _dft_consts_direct(seq_len)
        kern, scale = filt[seq_len]
        spec = _hyena_spectrum(kern, scale, consts, seq_len=seq_len)
        conv = _hy_longconv_lat if fft else _hy_longconv_direct
        y = z[0]
        for o in range(HY_ORDER):
            y = conv(y, z[o + 1], hy_bias, spec, consts, layer=layer, order=o, seq_len=seq_len,
                     out_dtype=BF16 if o == HY_ORDER - 1 else F32)
        outs.append(y.reshape(batch * seq_len, BRANCH_W))
    return outs


def _merge_kernel(h_ref, mod_ref, ng_ref, y0_ref, y1_ref, y2_ref, y3l_ref, y3c_ref, mw_ref, mb_ref, bw_ref,
                  o_ref, xn_ref, *, lat_tiles):
    i, c, br = pl.program_id(0), pl.program_id(1), pl.program_id(2)

    @pl.when((br == 0) & (c == 0))
    def _():
        xn = _rms(h_ref[...], ng_ref[2:3, :]) * (1.0 + mod_ref[4:5, :]) + mod_ref[3:4, :]
        xn_ref[...] = xn.astype(BF16)

    g = jax.nn.sigmoid(_dot(xn_ref[...], mw_ref[...]) + mb_ref[...])

    @pl.when(br == 0)
    def _():
        o_ref[...] = g * _dot(y0_ref[...], bw_ref[...])

    def accumulate(y_ref):
        o_ref[...] += g * _dot(y_ref[...], bw_ref[...])

    pl.when(br == 1)(lambda: accumulate(y1_ref))
    pl.when(br == 2)(lambda: accumulate(y2_ref))
    pl.when((br == 3) & (i < lat_tiles))(lambda: accumulate(y3l_ref))
    pl.when((br == 3) & (i >= lat_tiles))(lambda: accumulate(y3c_ref))


def _merge(h, mods, norm_gain, ys, merge_w, merge_b, branch_w, tiles):
    d = h.shape[1]
    tm, tn = ROW_TILE, 1024
    nc = d // tn
    l, lat_tiles = tiles.layer, tiles.lat_tiles
    depth = merge_b.shape[0]
    y_ret, y_s5, y_gla, y_hy_lat, y_hy_ctx = ys
    yspec = pl.BlockSpec((tm, BRANCH_W), lambda i, c, br: (i, 0))
    return pl.pallas_call(
        functools.partial(_merge_kernel, lat_tiles=lat_tiles),
        grid=(tiles.n_tiles, nc, N_BRANCH),
        in_specs=[pl.BlockSpec((tm, d), lambda i, c, br: (i, 0)),
                  tiles.mod_spec(d, 3),
                  tiles.gain_spec(d),
                  yspec, yspec, yspec,
                  pl.BlockSpec((tm, BRANCH_W), lambda i, c, br: (jnp.minimum(i, lat_tiles - 1), 0)),
                  pl.BlockSpec((tm, BRANCH_W), lambda i, c, br: (jnp.maximum(i - lat_tiles, 0), 0)),
                  pl.BlockSpec((None, d, tn), lambda i, c, br: (l, 0, br * nc + c)),
                  pl.BlockSpec((None, 1, tn), lambda i, c, br: (l, 0, br * nc + c)),
                  pl.BlockSpec((None, None, BRANCH_W, tn), lambda i, c, br: (l, br, 0, c))],
        out_specs=pl.BlockSpec((tm, tn), lambda i, c, br: (i, c)),
        out_shape=jax.ShapeDtypeStruct((tiles.n_tiles * tm, d), F32),
        scratch_shapes=[pltpu.VMEM((tm, d), BF16)],
        compiler_params=_cparams(("parallel", "arbitrary", "arbitrary")),
        name="merge",
    )(h, mods, norm_gain, y_ret, y_s5, y_gla, y_hy_lat, y_hy_ctx, merge_w, merge_b.reshape(depth, 1, -1), branch_w)


def _outproj_kernel(a_ref, h_ref, mod_ref, ng_ref, w_ref, o_ref):
    y = _dot(a_ref[...].astype(BF16), w_ref[...])
    o_ref[...] = h_ref[...] + mod_ref[5:6, :] * _rms(y, ng_ref[3:4, :])


def _outproj(acc, h, mods, norm_gain, w_out, tiles):
    d = h.shape[1]
    tm = ROW_TILE
    l = tiles.layer
    return pl.pallas_call(
        _outproj_kernel,
        grid=(tiles.n_tiles,),
        in_specs=[pl.BlockSpec((tm, d), lambda i: (i, 0)),
                  pl.BlockSpec((tm, d), lambda i: (i, 0)),
                  tiles.mod_spec(d, 1),
                  tiles.gain_spec(d),
                  pl.BlockSpec((None, d, d), lambda i: (l, 0, 0))],
        out_specs=pl.BlockSpec((tm, d), lambda i: (i, 0)),
        out_shape=jax.ShapeDtypeStruct((tiles.n_tiles * tm, d), F32),
        compiler_params=_cparams(("parallel",)),
        name="outproj",
    )(acc, h, mods, norm_gain, w_out)


def _rope_tables(batch, lat_len, ctx_len):
    rows = lat_len // GRID_W
    row = np.repeat(np.arange(rows, dtype=np.float32), GRID_W)
    col = np.tile(np.arange(GRID_W, dtype=np.float32), rows)
    n_freq = RET_DK // 4
    inv = (ROPE_BASE ** (-np.arange(n_freq, dtype=np.float32) / n_freq)).astype(np.float32)
    ang = np.concatenate([row[:, None] * inv, col[:, None] * inv], axis=-1).astype(np.float32)
    cos, sin = np.cos(ang), np.sin(ang)
    cos2 = np.concatenate([cos, cos], axis=-1)
    sin2 = np.concatenate([-sin, sin], axis=-1)
    ctx_rows = batch * ctx_len
    cos_t = np.concatenate([np.tile(cos2, (batch, 1)), np.ones((ctx_rows, RET_DK), np.float32)], axis=0)
    sin_t = np.concatenate([np.tile(sin2, (batch, 1)), np.zeros((ctx_rows, RET_DK), np.float32)], axis=0)
    return jnp.asarray(cos_t, F32), jnp.asarray(sin_t, F32)


def _reorder_w_in(w):
    pad = jnp.zeros(w.shape[:-1] + (P_COLS - w.shape[-1],), w.dtype)
    lr0 = COL_HY
    lr1 = lr0 + 2 * GLA_RANK
    return jnp.concatenate([w[..., :lr0], w[..., lr1:], w[..., lr0:lr1], pad], axis=-1)


def kernel(x, c, ctx, c_ctx, w_ada, b_ada, norm_gain, ffn_w_gate, ffn_w_up, ffn_w_down, w_in, ret_decay_logit, ret_norm_gain, s5_a_re, s5_a_im, s5_log_dt, s5_b_re, s5_b_im, s5_c_re, s5_c_im, s5_d, s5_glu_w, s5_glu_b, gla_gate_w, gla_gate_b, gla_norm_gain, hy_short_w, hy_short_b, hy_w1, hy_b1, hy_w2, hy_b2, hy_w3, hy_freq, hy_bias, branch_w, merge_w, merge_b, w_out):
    batch, lat_len, d = x.shape
    ctx_len = ctx.shape[1]
    depth = w_ada.shape[0]
    lat_rows = batch * lat_len
    tiles_per_batch = lat_len // ROW_TILE
    lat_tiles = batch * tiles_per_batch
    assert lat_len % ROW_TILE == 0 and (batch * ctx_len) % ROW_TILE == 0
    all_tiles = lat_tiles + batch * ctx_len // ROW_TILE
    seq_kw = dict(batch=batch, lat_len=lat_len, ctx_len=ctx_len)

    cond = jnp.zeros((8, d), F32).at[:batch].set(c).at[batch].set(c_ctx)
    mods = _ada(cond, w_ada, b_ada).reshape(depth, 8, N_MOD, d)
    cos_t, sin_t = _rope_tables(batch, lat_len, ctx_len)

    wg, wu, wd = ffn_w_gate.astype(BF16), ffn_w_up.astype(BF16), ffn_w_down.astype(BF16)
    w_in_b = _reorder_w_in(w_in).astype(BF16)
    merge_w_b, branch_w_b, w_out_b, glu_w_b = (t.astype(BF16) for t in (merge_w, branch_w, w_out, s5_glu_w))
    s5_tab = _s5_tables(s5_a_re, s5_a_im, s5_log_dt, s5_b_re, s5_b_im, s5_c_re, s5_c_im)

    h = jnp.concatenate([x.reshape(lat_rows, d), ctx.reshape(batch * ctx_len, d)], axis=0)
    for l in range(depth):
        full = _Tiles(l, lat_tiles, tiles_per_batch, all_tiles)
        tail = full if l < depth - 1 else _Tiles(l, lat_tiles, tiles_per_batch, lat_tiles)
        h = _ffn(h, mods, norm_gain, wg, wu, wd, full, idx=0)
        p = _inproj(h, mods, norm_gain, w_in_b, full)
        y_ret = _retention(p, cos_t, sin_t, ret_decay_logit, ret_norm_gain, layer=l, **seq_kw)
        y_s5 = _s5(p, s5_tab, s5_d, glu_w_b, s5_glu_b, layer=l, **seq_kw)
        y_gla = _gla(p, gla_gate_w, gla_gate_b, gla_norm_gain, layer=l, **seq_kw)
        filt = {n: _hyena_filter(n, hy_w1[l], hy_b1[l], hy_w2[l], hy_b2[l], hy_w3[l], hy_freq[l])
                for n in (lat_len, ctx_len)}
        y_hy_lat, y_hy_ctx = _hyena(p, filt, hy_short_w, hy_short_b, hy_bias, layer=l, **seq_kw)
        acc = _merge(h, mods, norm_gain, (y_ret, y_s5, y_gla, y_hy_lat, y_hy_ctx), merge_w_b, merge_b,
                     branch_w_b, tail)
        h = _outproj(acc, h, mods, norm_gain, w_out_b, tail)
        h = _ffn(h, mods, norm_gain, wg, wu, wd, tail, idx=1)
    return h.reshape(batch, lat_len, d)
```

```python
import functools
import math

import jax
import jax.numpy as jnp
import numpy as np
from jax import lax
from jax.experimental import pallas as pl
from jax.experimental.pallas import tpu as pltpu

F32 = jnp.float32
BF16 = jnp.bfloat16

D_MODEL = 2048
N_MOD = 9
D_FF = 5632
EPS = 1e-6
GRID_W = 64
BRANCH_W = 1024
N_BRANCH = 4

RET_HEADS, RET_DK, RET_DV, RET_CHUNK = 4, 128, 256, 128
ROPE_BASE = 10000.0
S5_GROUP, S5_GROUPS, S5_STATE = 16, 64, 64
GLA_HEADS, GLA_DK, GLA_DV, GLA_RANK, GLA_TAU, GLA_CHUNK = 4, 128, 256, 16, 16.0, 64
GLA_SUB = 16
RET_STEP_CHUNKS = 2
GLA_STEP_CHUNKS = 4
HY_ORDER, HY_SHORT, HY_BANDS, HY_HIDDEN = 2, 3, 16, 64
HY_TARGET, HY_FAST, HY_SLOW, HY_SHIFT = 1e-2, 0.3, 1.5, 0.05

LANES = 128
ROW_TILE = 512
VMEM_LIMIT = 56 * 1024 * 1024

P_COLS = 10368
COL_RET_Q, COL_RET_K, COL_RET_V, COL_RET_G = 0, 512, 1024, 2048
COL_S5 = 3072
COL_GLA_Q, COL_GLA_K, COL_GLA_V, COL_GLA_G = 4096, 4608, 5120, 6144
COL_HY = 7168
COL_LR = 10240

S5_T = 16
S5_BLK = LANES // S5_GROUP
S5_NBLK = S5_GROUPS // S5_BLK
S5_W = S5_BLK * S5_STATE
S5_LAT_TILE = 4096

FFT_N2 = 128
FFT_MIN_LEN = 2048
FFT_NB = 8
FFT_CW = 512
FFT_DATA_PASSES = 1


def _cparams(sem):
    return pltpu.CompilerParams(dimension_semantics=sem, vmem_limit_bytes=VMEM_LIMIT)


def _dot(a, b):
    return jnp.dot(a, b, preferred_element_type=F32)


def _dot_nt(a, b):
    return lax.dot_general(a, b, (((1,), (1,)), ((), ())), preferred_element_type=F32)


def _dot_tn(a, b):
    return lax.dot_general(a, b, (((0,), (0,)), ((), ())), preferred_element_type=F32)


def _split2(x):
    hi = x.astype(BF16)
    lo = (x - hi.astype(F32)).astype(BF16)
    return hi, lo


def _split3(x):
    hi = x.astype(BF16)
    r = x - hi.astype(F32)
    mid = r.astype(BF16)
    lo = (r - mid.astype(F32)).astype(BF16)
    return hi, mid, lo


def _operand(x, passes):
    return _split2(x) if passes == 3 else (x.astype(BF16), None)


def _const_operand(hi_ref, lo_ref, passes):
    return (hi_ref[...], lo_ref[...] if passes == 3 else None)


def _mm(a, b):
    out = _dot(a[0], b[0])
    if a[1] is not None and b[1] is not None:
        out = out + (_dot(a[0], b[1]) + _dot(a[1], b[0]))
    return out


def _cmm(ar, ai, br, bi):
    return _mm(ar, br) - _mm(ai, bi), _mm(ar, bi) + _mm(ai, br)


def _neg(a):
    return (-a[0], None if a[1] is None else -a[1])


def _rms(x, gain):
    return x * lax.rsqrt(jnp.mean(x * x, axis=-1, keepdims=True) + EPS) * gain


def _silu(x):
    return x * jax.nn.sigmoid(x)


def _log_sigmoid(x):
    return jnp.minimum(x, 0.0) - jnp.log1p(jnp.exp(-jnp.abs(x)))


def _mod_index(i, lat_tiles, tiles_per_batch):
    return jnp.where(i < lat_tiles, i // tiles_per_batch, lat_tiles // tiles_per_batch)


def _ada_kernel(c_ref, w_ref, b_ref, o_ref):
    cs = c_ref[...]
    o_ref[...] = _dot(_silu(cs).astype(BF16), w_ref[...].astype(BF16)) + b_ref[...]


def _ada(cond, w_ada, b_ada):
    depth, d, n = w_ada.shape
    tn = 1024
    return pl.pallas_call(
        _ada_kernel,
        grid=(depth, n // tn),
        in_specs=[pl.BlockSpec((8, d), lambda l, j: (0, 0)),
                  pl.BlockSpec((None, d, tn), lambda l, j: (l, 0, j)),
                  pl.BlockSpec((None, 1, tn), lambda l, j: (l, 0, j))],
        out_specs=pl.BlockSpec((None, 8, tn), lambda l, j: (l, 0, j)),
        out_shape=jax.ShapeDtypeStruct((depth, 8, n), F32),
        compiler_params=_cparams(("parallel", "parallel")),
        name="adaln",
    )(cond, w_ada, b_ada.reshape(depth, 1, n))


class _Tiles:
    def __init__(self, layer, lat_tiles, tiles_per_batch, n_tiles):
        self.layer, self.lat_tiles, self.tiles_per_batch, self.n_tiles = layer, lat_tiles, tiles_per_batch, n_tiles

    def mod_spec(self, d):
        l, lt, tpb = self.layer, self.lat_tiles, self.tiles_per_batch
        return pl.BlockSpec((None, None, N_MOD, d), lambda i, *_: (l, _mod_index(i, lt, tpb), 0, 0))

    def gain_spec(self, d):
        l = self.layer
        return pl.BlockSpec((None, 6, d), lambda i, *_: (l, 0, 0))


def _ffn_kernel(h_ref, mod_ref, ng_ref, wg_ref, wu_ref, wd_ref, o_ref, xn_ref, acc_ref, *, mbase, gbase):
    j = pl.program_id(1)

    @pl.when(j == 0)
    def _():
        x = h_ref[...]
        xn = _rms(x, ng_ref[gbase:gbase + 1, :]) * (1.0 + mod_ref[mbase + 1:mbase + 2, :]) + mod_ref[mbase:mbase + 1, :]
        xn_ref[...] = xn.astype(BF16)
        acc_ref[...] = jnp.zeros_like(acc_ref)

    xn = xn_ref[...]
    a = _silu(_dot(xn, wg_ref[...])) * _dot(xn, wu_ref[...])
    acc_ref[...] += _dot(a.astype(BF16), wd_ref[...])

    @pl.when(j == pl.num_programs(1) - 1)
    def _():
        r = _rms(acc_ref[...], ng_ref[gbase + 1:gbase + 2, :])
        o_ref[...] = h_ref[...] + 0.5 * mod_ref[mbase + 2:mbase + 3, :] * r


def _ffn(h, mods, norm_gain, wg, wu, wd, tiles, *, idx):
    d = h.shape[1]
    f = wg.shape[-1]
    tm, tf = ROW_TILE, 512
    l = tiles.layer
    return pl.pallas_call(
        functools.partial(_ffn_kernel, mbase=6 * idx, gbase=4 * idx),
        grid=(tiles.n_tiles, f // tf),
        in_specs=[pl.BlockSpec((tm, d), lambda i, j: (i, 0)),
                  tiles.mod_spec(d),
                  tiles.gain_spec(d),
                  pl.BlockSpec((None, None, d, tf), lambda i, j: (l, idx, 0, j)),
                  pl.BlockSpec((None, None, d, tf), lambda i, j: (l, idx, 0, j)),
                  pl.BlockSpec((None, None, tf, d), lambda i, j: (l, idx, j, 0))],
        out_specs=pl.BlockSpec((tm, d), lambda i, j: (i, 0)),
        out_shape=jax.ShapeDtypeStruct((tiles.n_tiles * tm, d), F32),
        scratch_shapes=[pltpu.VMEM((tm, d), BF16), pltpu.VMEM((tm, d), F32)],
        compiler_params=_cparams(("parallel", "arbitrary")),
        name="ffn",
    )(h, mods, norm_gain, wg, wu, wd)


def _inproj_kernel(h_ref, mod_ref, ng_ref, w_ref, o_ref, xn_ref):
    @pl.when(pl.program_id(1) == 0)
    def _():
        xn = _rms(h_ref[...], ng_ref[2:3, :]) * (1.0 + mod_ref[4:5, :]) + mod_ref[3:4, :]
        xn_ref[...] = xn.astype(BF16)

    o_ref[...] = _dot(xn_ref[...], w_ref[...])


def _inproj(h, mods, norm_gain, w, tiles):
    r, d = h.shape
    n = w.shape[-1]
    tm, tn = ROW_TILE, 1152
    l = tiles.layer
    return pl.pallas_call(
        _inproj_kernel,
        grid=(r // tm, n // tn),
        in_specs=[pl.BlockSpec((tm, d), lambda i, j: (i, 0)),
                  tiles.mod_spec(d),
                  tiles.gain_spec(d),
                  pl.BlockSpec((None, d, tn), lambda i, j: (l, 0, j))],
        out_specs=pl.BlockSpec((tm, tn), lambda i, j: (i, j)),
        out_shape=jax.ShapeDtypeStruct((r, n), F32),
        scratch_shapes=[pltpu.VMEM((tm, d), BF16)],
        compiler_params=_cparams(("parallel", "arbitrary")),
        name="inproj",
    )(h, mods, norm_gain, w)


def _chunk_row_block(b, j, *, lat_chunks, ctx_chunks, reverse, batch):
    ctx0 = batch * lat_chunks + b * ctx_chunks
    lat0 = b * lat_chunks
    jl = j - ctx_chunks
    if reverse:
        return jnp.where(j < ctx_chunks, ctx0 + (ctx_chunks - 1 - j), lat0 + (lat_chunks - 1 - jl))
    return jnp.where(j < ctx_chunks, ctx0 + j, lat0 + jl)


def _ret_kernel(*refs, reverse):
    if reverse:
        dl_ref, q_ref, k_ref, v_ref, cos_ref, sin_ref, of_ref, g_ref, gain_ref, o_ref, s_ref = refs
    else:
        dl_ref, q_ref, k_ref, v_ref, cos_ref, sin_ref, o_ref, s_ref = refs
    C = RET_CHUNK

    @pl.when(pl.program_id(1) == 0)
    def _():
        s_ref[...] = jnp.zeros_like(s_ref)

    ti = lax.broadcasted_iota(jnp.int32, (C, C), 0).astype(F32)
    si = lax.broadcasted_iota(jnp.int32, (C, C), 1).astype(F32)
    tr = lax.broadcasted_iota(jnp.int32, (C, RET_DK), 0).astype(F32)
    rel = (si - ti) if reverse else (ti - si)
    scale = RET_DK ** -0.5
    decays = []
    for h in range(RET_HEADS):
        lg_cc = _log_sigmoid(jnp.full((C, C), dl_ref[h], F32))
        lg_cd = _log_sigmoid(jnp.full((C, RET_DK), dl_ref[h], F32))
        d_intra = jnp.where(rel >= 0, jnp.exp(lg_cc * jnp.maximum(rel, 0.0)), 0.0)
        if reverse:
            d_q = jnp.exp((C - tr) * lg_cd)
            d_k = jnp.exp(tr * lg_cd)
        else:
            d_q = jnp.exp((tr + 1.0) * lg_cd)
            d_k = jnp.exp((C - 1.0 - tr) * lg_cd)
        decays.append((d_intra, d_q, d_k, jnp.exp(C * lg_cd[0:1, 0:1])))

    order = range(RET_STEP_CHUNKS - 1, -1, -1) if reverse else range(RET_STEP_CHUNKS)
    for ci in order:
        rows = slice(ci * C, (ci + 1) * C)
        cos, sin = cos_ref[rows, :], sin_ref[rows, :]
        for h in range(RET_HEADS):
            d_intra, d_q, d_k, d_c = decays[h]
            hk = slice(h * RET_DK, (h + 1) * RET_DK)
            hv = slice(h * RET_DV, (h + 1) * RET_DV)
            qh, kh = q_ref[rows, hk], k_ref[rows, hk]
            vh = v_ref[rows, hv].astype(BF16)
            qr = qh * cos + pltpu.roll(qh, RET_DK // 2, axis=1) * sin
            kr = (kh * cos + pltpu.roll(kh, RET_DK // 2, axis=1) * sin) * scale

            scores = _dot_nt(qr.astype(BF16), kr.astype(BF16)) * d_intra
            s_old = s_ref[h]
            o = _dot(scores.astype(BF16), vh) + _dot((qr * d_q).astype(BF16), s_old.astype(BF16))
            s_ref[h] = s_old * d_c + _dot_tn((kr * d_k).astype(BF16), vh)

            if reverse:
                o = o + of_ref[rows, hv]
                mu = jnp.mean(o, axis=-1, keepdims=True)
                oc = o - mu
                var = jnp.mean(oc * oc, axis=-1, keepdims=True)
                on = oc * lax.rsqrt(var + EPS) * gain_ref[:, hv]
                o = _silu(g_ref[rows, hv]) * on
            o_ref[rows, hv] = o.astype(o_ref.dtype)


def _retention(p, cos_t, sin_t, decay_logit, norm_gain, *, layer, batch, lat_len, ctx_len):
    r = p.shape[0]
    C = RET_CHUNK * RET_STEP_CHUNKS
    lat_chunks, ctx_chunks = lat_len // C, ctx_len // C
    steps = lat_chunks + ctx_chunks

    def call(reverse, extra):
        rb = functools.partial(_chunk_row_block, lat_chunks=lat_chunks, ctx_chunks=ctx_chunks,
                               reverse=reverse, batch=batch)
        row = lambda cb: (lambda b, j: (rb(b, j), cb))
        in_specs = [pl.BlockSpec(memory_space=pltpu.SMEM),
                    pl.BlockSpec((C, 512), row(COL_RET_Q // 512)),
                    pl.BlockSpec((C, 512), row(COL_RET_K // 512)),
                    pl.BlockSpec((C, 1024), row(COL_RET_V // 1024)),
                    pl.BlockSpec((C, RET_DK), row(0)),
                    pl.BlockSpec((C, RET_DK), row(0))]
        args = [decay_logit[layer, 1 if reverse else 0], p, p, p, cos_t, sin_t]
        if reverse:
            in_specs += [pl.BlockSpec((C, 1024), row(0)),
                         pl.BlockSpec((C, 1024), row(COL_RET_G // 1024)),
                         pl.BlockSpec((None, 1, 1024), lambda b, j: (layer, 0, 0))]
            args += [extra, p, norm_gain.reshape(-1, 1, 1024)]
        return pl.pallas_call(
            functools.partial(_ret_kernel, reverse=reverse),
            grid=(batch, steps),
            in_specs=in_specs,
            out_specs=pl.BlockSpec((C, 1024), row(0)),
            out_shape=jax.ShapeDtypeStruct((r, 1024), BF16 if reverse else F32),
            scratch_shapes=[pltpu.VMEM((RET_HEADS, RET_DK, RET_DV), F32)],
            compiler_params=_cparams(("parallel", "arbitrary")),
            name="retention_bwd" if reverse else "retention_fwd",
        )(*args)

    o_f = call(False, None)
    return call(True, o_f)


def _gla_kernel(*refs, reverse):
    if reverse:
        q_ref, k_ref, v_ref, a_ref, gw_ref, gb_ref, of_ref, g_ref, gain_ref, o_ref, s_ref = refs
    else:
        q_ref, k_ref, v_ref, a_ref, gw_ref, gb_ref, o_ref, s_ref = refs
    C, SB = GLA_CHUNK, GLA_SUB

    @pl.when(pl.program_id(1) == 0)
    def _():
        s_ref[...] = jnp.zeros_like(s_ref)

    la_all = _log_sigmoid(_dot(a_ref[...].astype(BF16), gw_ref[...]) + gb_ref[...]) / GLA_TAU
    ti = lax.broadcasted_iota(jnp.int32, (C, C), 0)
    si = lax.broadcasted_iota(jnp.int32, (C, C), 1)
    tri = jnp.where((si >= ti) if reverse else (si <= ti), 1.0, 0.0).astype(BF16)
    ones = jnp.ones((C, LANES), BF16)
    srow = lax.broadcasted_iota(jnp.int32, (C, GLA_DK), 0)
    scale = GLA_DK ** -0.5
    end = 0 if reverse else C - 1

    order = range(GLA_STEP_CHUNKS - 1, -1, -1) if reverse else range(GLA_STEP_CHUNKS)
    for ci in order:
        c0 = ci * C
        rows = slice(c0, c0 + C)
        la3 = _split3(la_all[rows])
        bcum = _dot(tri, la3[0]) + (_dot(tri, la3[1]) + _dot(tri, la3[2]))
        for h in range(GLA_HEADS):
            sl = slice(h * GLA_DK, (h + 1) * GLA_DK)
            cs = slice(h * GLA_DV, (h + 1) * GLA_DV)
            b = bcum[:, sl]
            q = q_ref[rows, sl]
            k = k_ref[rows, sl] * scale
            v = v_ref[rows, cs].astype(BF16)
            s_old = s_ref[h]
            b_end = b[end:end + 1, :]
            o_inter = _dot((q * jnp.exp(b)).astype(BF16), s_old.astype(BF16))
            col = _dot_tn(la3[0][:, sl], ones) + (_dot_tn(la3[1][:, sl], ones) + _dot_tn(la3[2][:, sl], ones))
            dec_col = jnp.exp(jnp.concatenate([col, col], axis=1))
            s_ref[h] = s_old * dec_col + _dot_tn((k * jnp.exp(b_end - b)).astype(BF16), v)

            for blk in range(C // SB):
                r0 = blk * SB
                ref_row = b[r0 + SB - 1:r0 + SB, :] if reverse else b[r0:r0 + 1, :]
                reach = (srow >= r0) if reverse else (srow < r0 + SB)
                qt = (q[r0:r0 + SB] * jnp.exp(b[r0:r0 + SB] - ref_row)).astype(BF16)
                kt = (k * jnp.exp(jnp.where(reach, ref_row - b, 0.0))).astype(BF16)
                sc = _dot_nt(qt, kt)
                tt = lax.broadcasted_iota(jnp.int32, sc.shape, 0) + r0
                ss = lax.broadcasted_iota(jnp.int32, sc.shape, 1)
                sc = jnp.where((ss >= tt) if reverse else (ss <= tt), sc, 0.0)
                o = _dot(sc.astype(BF16), v) + o_inter[r0:r0 + SB]
                orow = slice(c0 + r0, c0 + r0 + SB)
                if reverse:
                    o = o + of_ref[orow, cs]
                    on = o * lax.rsqrt(jnp.mean(o * o, axis=-1, keepdims=True) + EPS) * gain_ref[:, cs]
                    o = _silu(g_ref[orow, cs]) * on
                o_ref[orow, cs] = o.astype(o_ref.dtype)


def _gla(p, gate_w, gate_b, norm_gain, *, layer, batch, lat_len, ctx_len):
    r = p.shape[0]
    C = GLA_CHUNK * GLA_STEP_CHUNKS
    lat_chunks, ctx_chunks = lat_len // C, ctx_len // C
    steps = lat_chunks + ctx_chunks
    hk = GLA_HEADS * GLA_DK
    depth = gate_w.shape[0]
    gw = jnp.zeros((depth, 2, LANES, hk), F32)
    for d in range(2):
        gw = gw.at[:, d, d * GLA_RANK:(d + 1) * GLA_RANK].set(gate_w[:, d])
    gw = gw.astype(BF16)

    def call(reverse, extra):
        d = 1 if reverse else 0
        rb = functools.partial(_chunk_row_block, lat_chunks=lat_chunks, ctx_chunks=ctx_chunks,
                               reverse=reverse, batch=batch)
        row = lambda cb: (lambda b, j: (rb(b, j), cb))
        in_specs = [pl.BlockSpec((C, 512), row(COL_GLA_Q // 512)),
                    pl.BlockSpec((C, 512), row(COL_GLA_K // 512)),
                    pl.BlockSpec((C, 1024), row(COL_GLA_V // 1024)),
                    pl.BlockSpec((C, LANES), row(COL_LR // LANES)),
                    pl.BlockSpec((None, None, LANES, hk), lambda b, j: (layer, d, 0, 0)),
                    pl.BlockSpec((None, None, 1, hk), lambda b, j: (layer, d, 0, 0))]
        args = [p, p, p, p, gw, gate_b.reshape(depth, 2, 1, hk)]
        if reverse:
            in_specs += [pl.BlockSpec((C, 1024), row(0)),
                         pl.BlockSpec((C, 1024), row(COL_GLA_G // 1024)),
                         pl.BlockSpec((None, 1, 1024), lambda b, j: (layer, 0, 0))]
            args += [extra, p, norm_gain.reshape(depth, 1, 1024)]
        return pl.pallas_call(
            functools.partial(_gla_kernel, reverse=reverse),
            grid=(batch, steps),
            in_specs=in_specs,
            out_specs=pl.BlockSpec((C, 1024), row(0)),
            out_shape=jax.ShapeDtypeStruct((r, 1024), BF16 if reverse else F32),
            scratch_shapes=[pltpu.VMEM((GLA_HEADS, GLA_DK, GLA_DV), F32)],
            compiler_params=_cparams(("parallel", "arbitrary")),
            name="gla_bwd" if reverse else "gla_fwd",
        )(*args)

    o_f = call(False, None)
    return call(True, o_f)


def _s5_tables(a_re, a_im, log_dt, b_re, b_im, c_re, c_im):
    T, P, K = S5_T, S5_STATE, S5_GROUP
    lead = a_re.shape[:-2]
    nl = len(lead)
    dt = jnp.exp(log_dt)[..., None]
    mag = jnp.exp(a_re * dt)
    ab_re, ab_im = mag * jnp.cos(a_im * dt), mag * jnp.sin(a_im * dt)
    den = a_re * a_re + a_im * a_im
    nr = ab_re - 1.0
    f_re = (nr * a_re + ab_im * a_im) / den
    f_im = (ab_im * a_re - nr * a_im) / den
    bb_re = f_re[..., None] * b_re - f_im[..., None] * b_im
    bb_im = f_re[..., None] * b_im + f_im[..., None] * b_re

    def power(n):
        n = jnp.asarray(n, F32)[:, None, None]
        m = jnp.exp(n * (a_re * dt)[..., None, :, :])
        ang = n * (a_im * dt)[..., None, :, :]
        return m * jnp.cos(ang), m * jnp.sin(ang)

    eye = jnp.eye(S5_BLK, dtype=F32)

    def blockdiag_in(x):
        x = x.reshape(*lead, T, S5_NBLK, S5_BLK, P, K)
        x = jnp.moveaxis(x, nl, nl + 1)
        x = jnp.swapaxes(x, -1, -2)
        y = x[..., :, :, None, :] * eye[:, None, :, None]
        return y.reshape(*lead, S5_NBLK, T, LANES, S5_W)

    pr, pi = power(np.arange(T - 1, -1, -1))
    bre, bim = bb_re[..., None, :, :, :], bb_im[..., None, :, :, :]
    w_re = pr[..., None] * bre - pi[..., None] * bim
    w_im = pr[..., None] * bim + pi[..., None] * bre
    w_end = jnp.concatenate([blockdiag_in(w_re), blockdiag_in(w_im)], axis=-1).astype(BF16)

    qr, qi = power(np.arange(T))
    lb_re = qr[..., None] * bre - qi[..., None] * bim
    lb_im = qr[..., None] * bim + qi[..., None] * bre
    kj = (jnp.einsum('...gop,...jgpi->...jgio', c_re, lb_re)
          - jnp.einsum('...gop,...jgpi->...jgio', c_im, lb_im))
    kj = kj.reshape(*lead, T, S5_NBLK, S5_BLK, K, K)
    kj = jnp.moveaxis(kj, nl, nl + 1)
    kj = kj[..., :, :, None, :] * eye[:, None, :, None]
    kj = kj.reshape(*lead, S5_NBLK, T, LANES, LANES).astype(BF16)

    def blockdiag_out(x):
        x = jnp.swapaxes(x.reshape(*lead, S5_NBLK, S5_BLK, K, P), -1, -2)
        y = x[..., :, :, None, :] * eye[:, None, :, None]
        return y.reshape(*lead, S5_NBLK, S5_W, LANES)

    c_out = jnp.concatenate([blockdiag_out(c_re), -blockdiag_out(c_im)], axis=-2).astype(BF16)

    sr, si = power(np.arange(1, T + 1))
    lam_pow = jnp.concatenate([sr.reshape(*lead, T, S5_NBLK, 1, S5_W), si.reshape(*lead, T, S5_NBLK, 1, S5_W)], axis=-1)
    lam_pow = jnp.moveaxis(lam_pow, nl, nl + 1)
    return w_end, kj, c_out, lam_pow


def _s5_kernel(u_ref, x0_ref, wend_ref, kj_ref, cout_ref, lam_ref, y_ref, xf_ref,
               ubuf_ref, sloc_ref, xin_ref, x_ref, *, reverse, n):
    T, W = S5_T, S5_W

    @pl.when(pl.program_id(2) == 0)
    def _():
        x_ref[...] = x0_ref[...]

    off = lambda r: (T - 1 - r) if reverse else r
    for r in range(T):
        ubuf_ref[r] = u_ref[pl.ds(off(r), n, stride=T), :].astype(BF16)

    sloc = _dot(ubuf_ref[0], wend_ref[0])
    for r in range(1, T):
        sloc = sloc + _dot(ubuf_ref[r], wend_ref[r])
    sloc_ref[...] = sloc

    lt = lam_ref[T - 1]
    ltr, lti = lt[:, :W], lt[:, W:]

    def body(i, x):
        c = (n - 1 - i) if reverse else i
        xin_ref[pl.ds(c, 1), :] = x
        s = sloc_ref[pl.ds(c, 1), :]
        xr, xi = x[:, :W], x[:, W:]
        return jnp.concatenate([ltr * xr - lti * xi + s[:, :W], ltr * xi + lti * xr + s[:, W:]], axis=1)

    x_fin = lax.fori_loop(0, n, body, x_ref[...])
    x_ref[...] = x_fin
    xf_ref[...] = x_fin

    xin = xin_ref[...]
    xr, xi = xin[:, :W], xin[:, W:]
    cout = cout_ref[...]
    for r in range(T):
        lp = lam_ref[r]
        pr, pi = lp[:, :W], lp[:, W:]
        z = jnp.concatenate([pr * xr - pi * xi, pr * xi + pi * xr], axis=1)
        y = _dot(z.astype(BF16), cout)
        for j in range(r + 1):
            y = y + _dot(ubuf_ref[r - j], kj_ref[j])
        y_ref[pl.ds(off(r), n, stride=T), :] = y


def _s5_scan(p, tables, x0, *, layer, reverse, row0, rows_per_batch, tile, batch):
    w_end, kj, c_out, lam_pow = tables
    T, W = S5_T, S5_W
    n = tile // T
    steps = rows_per_batch // tile
    blk0 = row0 // tile
    d = 1 if reverse else 0

    def urow(kb, b, j):
        jj = (steps - 1 - j) if reverse else j
        return blk0 + b * steps + jj

    tab = lambda *tail: pl.BlockSpec((None, None, None) + tail, lambda kb, b, j: (layer, d, kb) + (0,) * len(tail))
    y, xf = pl.pallas_call(
        functools.partial(_s5_kernel, reverse=reverse, n=n),
        grid=(S5_NBLK, batch, steps),
        in_specs=[pl.BlockSpec((tile, LANES), lambda kb, b, j: (urow(kb, b, j), COL_S5 // LANES + kb)),
                  pl.BlockSpec((None, 1, 2 * W), lambda kb, b, j: (b * S5_NBLK + kb, 0, 0)),
                  tab(T, LANES, 2 * W), tab(T, LANES, LANES), tab(2 * W, LANES), tab(T, 1, 2 * W)],
        out_specs=[pl.BlockSpec((tile, LANES), lambda kb, b, j: (urow(kb, b, j) - blk0, kb)),
                   pl.BlockSpec((None, 1, 2 * W), lambda kb, b, j: (b * S5_NBLK + kb, 0, 0))],
        out_shape=[jax.ShapeDtypeStruct((batch * rows_per_batch, BRANCH_W), F32),
                   jax.ShapeDtypeStruct((batch * S5_NBLK, 1, 2 * W), F32)],
        scratch_shapes=[pltpu.VMEM((T, n, LANES), BF16), pltpu.VMEM((n, 2 * W), F32),
                        pltpu.VMEM((n, 2 * W), F32), pltpu.VMEM((1, 2 * W), F32)],
        compiler_params=_cparams(("parallel", "parallel", "arbitrary")),
        name="s5_scan",
    )(p, x0, w_end, kj, c_out, lam_pow)
    return y, xf


def _gelu_tanh(x):
    return 0.5 * x * (1.0 + jnp.tanh(math.sqrt(2.0 / math.pi) * (x + 0.044715 * (x * x * x))))


def _s5_out_kernel(ylf_ref, ylb_ref, ycf_ref, ycb_ref, u_ref, d_ref, w_ref, b_ref, o_ref, *, lat_tiles):
    def emit(yf_ref, yb_ref):
        y = _gelu_tanh(yf_ref[...] + yb_ref[...] + d_ref[...] * u_ref[...])
        o_ref[...] = (y * jax.nn.sigmoid(_dot(y.astype(BF16), w_ref[...]) + b_ref[...])).astype(o_ref.dtype)

    i = pl.program_id(0)
    pl.when(i < lat_tiles)(lambda: emit(ylf_ref, ylb_ref))
    pl.when(i >= lat_tiles)(lambda: emit(ycf_ref, ycb_ref))


def _s5(p, tables, s5_d, glu_w, glu_b, *, layer, batch, lat_len, ctx_len):
    r = p.shape[0]
    lat_rows = batch * lat_len
    zeros = jnp.zeros((batch * S5_NBLK, 1, 2 * S5_W), F32)
    ys = []
    for d in range(2):
        kw = dict(layer=layer, reverse=bool(d), batch=batch)
        y_c, x_c = _s5_scan(p, tables, zeros, row0=lat_rows, rows_per_batch=ctx_len, tile=ctx_len, **kw)
        y_l, _ = _s5_scan(p, tables, x_c, row0=0, rows_per_batch=lat_len, tile=min(lat_len, S5_LAT_TILE), **kw)
        ys += [y_l, y_c]
    tm = ROW_TILE
    lat_tiles = lat_rows // tm
    depth = s5_d.shape[0]
    lat = pl.BlockSpec((tm, BRANCH_W), lambda i: (jnp.minimum(i, lat_tiles - 1), 0))
    ctx = pl.BlockSpec((tm, BRANCH_W), lambda i: (jnp.maximum(i - lat_tiles, 0), 0))
    return pl.pallas_call(
        functools.partial(_s5_out_kernel, lat_tiles=lat_tiles),
        grid=(r // tm,),
        in_specs=[lat, lat, ctx, ctx,
                  pl.BlockSpec((tm, BRANCH_W), lambda i: (i, COL_S5 // BRANCH_W)),
                  pl.BlockSpec((None, 1, BRANCH_W), lambda i: (layer, 0, 0)),
                  pl.BlockSpec((None, BRANCH_W, BRANCH_W), lambda i: (layer, 0, 0)),
                  pl.BlockSpec((None, 1, BRANCH_W), lambda i: (layer, 0, 0))],
        out_specs=pl.BlockSpec((tm, BRANCH_W), lambda i: (i, 0)),
        out_shape=jax.ShapeDtypeStruct((r, BRANCH_W), BF16),
        compiler_params=_cparams(("parallel",)),
        name="s5_out",
    )(ys[0], ys[2], ys[1], ys[3], p, s5_d.reshape(depth, 1, -1), glu_w, glu_b.reshape(depth, 1, -1))


def _hy_short_kernel(z_ref, w_ref, b_ref, o_ref):
    z = z_ref[...]
    n = z.shape[0]
    t = lax.broadcasted_iota(jnp.int32, z.shape, 0)
    zm = jnp.where(t == 0, 0.0, pltpu.roll(z, 1, axis=0))
    zp = jnp.where(t == n - 1, 0.0, pltpu.roll(z, n - 1, axis=0))
    o_ref[...] = w_ref[0:1, :] * zm + w_ref[1:2, :] * z + w_ref[2:3, :] * zp + b_ref[...]


def _hy_short(p, w, b, *, layer, row0, seq_len, batch):
    cb = 256
    nb = BRANCH_W // cb
    blk0 = row0 // seq_len
    depth = w.shape[0]
    return pl.pallas_call(
        _hy_short_kernel,
        grid=(HY_ORDER + 1, batch, nb),
        in_specs=[pl.BlockSpec((seq_len, cb), lambda s, bb, c: (blk0 + bb, COL_HY // cb + s * nb + c)),
                  pl.BlockSpec((None, HY_SHORT, cb), lambda s, bb, c: (layer, 0, s * nb + c)),
                  pl.BlockSpec((None, 1, cb), lambda s, bb, c: (layer, 0, s * nb + c))],
        out_specs=pl.BlockSpec((None, None, seq_len, cb), lambda s, bb, c: (s, bb, 0, c)),
        out_shape=jax.ShapeDtypeStruct((HY_ORDER + 1, batch, seq_len, BRANCH_W), F32),
        compiler_params=_cparams(("parallel", "parallel", "parallel")),
        name="hyena_short",
    )(p, w, b.reshape(depth, 1, -1))


def _dft(n_out, n_in, n, sign):
    k = np.arange(n_out)[:, None].astype(np.float64)
    m = np.arange(n_in)[None, :].astype(np.float64)
    ang = sign * 2.0 * np.pi * ((k * m) % n) / n
    return np.cos(ang), np.sin(ang)


def _const_split(x):
    return _split2(jnp.asarray(x, F32))


def _const_spec(shape):
    return pl.BlockSpec(shape, lambda *_: (0,) * len(shape))


def _fft_consts(seq_len):
    n = 2 * seq_len
    n1 = n // FFT_N2
    n1h = seq_len // FFT_N2
    c1, s1 = _dft(n1, n1, n1, -1.0)
    f1_full = _const_split(c1) + _const_split(s1)
    f1 = _const_split(c1[:, :n1h]) + _const_split(s1[:, :n1h])
    c2, s2 = _dft(FFT_N2, FFT_N2, FFT_N2, -1.0)
    f2 = _const_split(c2) + _const_split(s2)
    cg, sg = _dft(n1h, n1, n1, 1.0)
    g1 = _const_split(cg) + _const_split(sg)
    k1 = np.arange(n1)[:, None].astype(np.float64)
    n2 = np.arange(FFT_N2)[None, :].astype(np.float64)
    ang = -2.0 * np.pi * (k1 * n2) / n
    twr = jnp.asarray(np.repeat(np.cos(ang)[:, :, None], LANES, axis=2), F32)
    twi = jnp.asarray(np.repeat(np.sin(ang)[:, :, None], LANES, axis=2), F32)
    return dict(f1_full=f1_full, f1=f1, f2=f2, g1=g1, twr=twr, twi=twi)


def _dft_consts_direct(seq_len):
    n = 2 * seq_len
    c, s = _dft(n, n, n, -1.0)
    full = _const_split(c) + _const_split(s)
    fwd = _const_split(c[:, :seq_len]) + _const_split(s[:, :seq_len])
    c, s = _dft(seq_len, n, n, 1.0)
    inv = _const_split(c) + _const_split(s)
    return dict(full=full, fwd=fwd, inv=inv)


def _hyena_filter(L, w1, b1, w2, b2, w3, freq):
    pos = np.concatenate([np.arange(L), [0], np.arange(L - 1, 0, -1)])
    fwd = jnp.asarray((np.arange(2 * L) < L)[:, None, None], F32)
    bwd = jnp.asarray((np.arange(2 * L) > L)[:, None, None], F32)
    t = jnp.linspace(0.0, 1.0, L, dtype=F32)[pos]
    w = (2.0 * math.pi * jnp.arange(L, dtype=F32) / L)[pos]
    bands = jnp.linspace(1e-4, HY_BANDS - 1.0, HY_BANDS, dtype=F32)
    ph = w[:, None] * bands[None, :]
    z = jnp.concatenate([t[:, None], jnp.cos(ph), -jnp.sin(ph)], axis=-1)
    h = jnp.sin(freq[0] * (z @ w1 + b1))
    h = jnp.sin(freq[1] * (h @ w2 + b2))
    h = (h @ w3).reshape(2 * L, HY_ORDER, 2, BRANCH_W)
    deltas = jnp.abs(jnp.linspace(math.log(HY_TARGET) / HY_FAST, math.log(HY_TARGET) / HY_SLOW, BRANCH_W, dtype=F32))
    window = jnp.exp(-t[:, None] * deltas[None, :]) + HY_SHIFT
    kern = (h[:, :, 0] * fwd + h[:, :, 1] * bwd) * window[:, None, :]
    scale = lax.rsqrt(jnp.sum(kern * kern, axis=0) + EPS)
    return kern.reshape(2 * L, HY_ORDER * BRANCH_W), scale.reshape(HY_ORDER, 1, BRANCH_W)


def _spec_fwd1_kernel(z_ref, frh_ref, frl_ref, fih_ref, fil_ref, ar_ref, ai_ref):
    fr, fi = (frh_ref[...], frl_ref[...]), (fih_ref[...], fil_ref[...])
    for j in range(FFT_NB):
        z = _split2(z_ref[:, j, :])
        ar_ref[:, j, :] = _mm(fr, z)
        ai_ref[:, j, :] = _mm(fi, z)


def _spec_fwd2_kernel(ar_ref, ai_ref, twr_ref, twi_ref, frh_ref, frl_ref, fih_ref, fil_ref, sc_ref,
                      hr_ref, hi_ref, *, reps):
    twr = jnp.concatenate([twr_ref[...]] * reps, axis=1)
    twi = jnp.concatenate([twi_ref[...]] * reps, axis=1)
    ar, ai = ar_ref[...], ai_ref[...]
    xr, xi = ar * twr - ai * twi, ar * twi + ai * twr
    sr, si = _cmm((frh_ref[...], frl_ref[...]), (fih_ref[...], fil_ref[...]), _split2(xr), _split2(xi))
    hr_ref[...] = sr * sc_ref[...]
    hi_ref[...] = si * sc_ref[...]


def _spec_direct_kernel(z_ref, frh_ref, frl_ref, fih_ref, fil_ref, sc_ref, hr_ref, hi_ref):
    z = _split2(z_ref[...])
    hr_ref[...] = _mm((frh_ref[...], frl_ref[...]), z) * sc_ref[...]
    hi_ref[...] = _mm((fih_ref[...], fil_ref[...]), z) * sc_ref[...]


def _hyena_spectrum(kern, scale, consts, *, seq_len):
    C = BRANCH_W
    n = 2 * seq_len
    if seq_len < FFT_MIN_LEN:
        cb = 256
        nb = C // cb
        return pl.pallas_call(
            _spec_direct_kernel,
            grid=(HY_ORDER, nb),
            in_specs=[pl.BlockSpec((n, cb), lambda o, i: (0, o * nb + i))] + [_const_spec((n, n))] * 4
                     + [pl.BlockSpec((None, 1, cb), lambda o, i: (o, 0, i))],
            out_specs=[pl.BlockSpec((None, n, cb), lambda o, i: (o, 0, i))] * 2,
            out_shape=[jax.ShapeDtypeStruct((HY_ORDER, n, C), F32)] * 2,
            compiler_params=_cparams(("parallel", "parallel")),
            name="hyena_spec_direct",
        )(kern, *consts['full'], scale)
    n1 = n // FFT_N2
    cw = FFT_CW
    nc = C // cw
    ar, ai = pl.pallas_call(
        _spec_fwd1_kernel,
        grid=(HY_ORDER, FFT_N2 // FFT_NB, nc),
        in_specs=[pl.BlockSpec((n1, FFT_NB, cw), lambda o, i, c: (0, i, o * nc + c))] + [_const_spec((n1, n1))] * 4,
        out_specs=[pl.BlockSpec((None, n1, FFT_NB, cw), lambda o, i, c: (o, 0, i, c))] * 2,
        out_shape=[jax.ShapeDtypeStruct((HY_ORDER, n1, FFT_N2, C), F32)] * 2,
        compiler_params=_cparams(("parallel", "parallel", "parallel")),
        name="hyena_spec_fwd1",
    )(kern.reshape(n1, FFT_N2, HY_ORDER * C), *consts['f1_full'])
    blk = pl.BlockSpec((None, None, FFT_N2, C), lambda o, i: (o, i, 0, 0))
    tw = pl.BlockSpec((None, FFT_N2, LANES), lambda o, i: (i, 0, 0))
    return pl.pallas_call(
        functools.partial(_spec_fwd2_kernel, reps=C // LANES),
        grid=(HY_ORDER, n1),
        in_specs=[blk, blk, tw, tw] + [_const_spec((FFT_N2, FFT_N2))] * 4
                 + [pl.BlockSpec((None, 1, C), lambda o, i: (o, 0, 0))],
        out_specs=[blk, blk],
        out_shape=[jax.ShapeDtypeStruct((HY_ORDER, n1, FFT_N2, C), F32)] * 2,
        compiler_params=_cparams(("parallel", "parallel")),
        name="hyena_spec_fwd2",
    )(ar, ai, consts['twr'], consts['twi'], *consts['f2'], scale)


def _fft_fwd1_kernel(z_ref, frh_ref, frl_ref, fih_ref, fil_ref, ar_ref, ai_ref, *, passes):
    fr, fi = _const_operand(frh_ref, frl_ref, passes), _const_operand(fih_ref, fil_ref, passes)
    for j in range(FFT_NB):
        ar, ai = _cmm(fr, fi, _operand(z_ref[0, :, j, :], passes), _operand(z_ref[1, :, j, :], passes))
        ar_ref[:, j, :] = ar
        ai_ref[:, j, :] = ai


def _fft_mid_kernel(ar_ref, ai_ref, twr_ref, twi_ref, hr_ref, hi_ref, frh_ref, frl_ref, fih_ref, fil_ref,
                    br_ref, bi_ref, *, reps, passes):
    twr = jnp.concatenate([twr_ref[...]] * reps, axis=1)
    twi = jnp.concatenate([twi_ref[...]] * reps, axis=1)
    ar, ai = ar_ref[...], ai_ref[...]
    xr, xi = ar * twr - ai * twi, ar * twi + ai * twr
    fr, fi = _const_operand(frh_ref, frl_ref, passes), _const_operand(fih_ref, fil_ref, passes)
    sr, si = _cmm(fr, fi, _operand(xr, passes), _operand(xi, passes))
    hr, hi = hr_ref[...], hi_ref[...]
    yr, yi = sr * hr - si * hi, sr * hi + si * hr
    br, bi = _cmm(fr, _neg(fi), _operand(yr, passes), _operand(yi, passes))
    br_ref[...] = br * twr + bi * twi
    bi_ref[...] = bi * twr - br * twi


def _fft_inv1_kernel(br_ref, bi_ref, grh_ref, grl_ref, gih_ref, gil_ref, gate_ref, y_ref, bias_ref, o_ref, *,
                     inv_n, passes):
    gr, gi = _const_operand(grh_ref, grl_ref, passes), _const_operand(gih_ref, gil_ref, passes)
    bias = bias_ref[...]
    for j in range(FFT_NB):
        out = _cmm(gr, gi, _operand(br_ref[:, j, :], passes), _operand(bi_ref[:, j, :], passes))
        for b in range(2):
            o_ref[b, :, j, :] = (gate_ref[b, :, j, :] * (out[b] * inv_n + y_ref[b, :, j, :] * bias)).astype(o_ref.dtype)


def _hy_longconv_lat(y, gate, bias, spec, consts, *, layer, order, seq_len, out_dtype):
    n1 = 2 * seq_len // FFT_N2
    n1h = seq_len // FFT_N2
    C = y.shape[-1]
    passes = FFT_DATA_PASSES
    cw = FFT_CW
    grid = (FFT_N2 // FFT_NB, C // cw)
    sig = pl.BlockSpec((2, n1h, FFT_NB, cw), lambda i, c: (0, 0, i, c))
    mid = pl.BlockSpec((n1, FFT_NB, cw), lambda i, c: (0, i, c))
    yv = y.reshape(2, n1h, FFT_N2, C)
    ar, ai = pl.pallas_call(
        functools.partial(_fft_fwd1_kernel, passes=passes),
        grid=grid,
        in_specs=[sig] + [_const_spec((n1, n1h))] * 4,
        out_specs=[mid, mid],
        out_shape=[jax.ShapeDtypeStruct((n1, FFT_N2, C), F32)] * 2,
        compiler_params=_cparams(("parallel", "parallel")),
        name="hyena_fft_fwd1",
    )(yv, *consts['f1'])
    blk = pl.BlockSpec((None, FFT_N2, C), lambda i: (i, 0, 0))
    sblk = pl.BlockSpec((None, None, FFT_N2, C), lambda i: (order, i, 0, 0))
    tw = pl.BlockSpec((None, FFT_N2, LANES), lambda i: (i, 0, 0))
    br, bi = pl.pallas_call(
        functools.partial(_fft_mid_kernel, reps=C // LANES, passes=passes),
        grid=(n1,),
        in_specs=[blk, blk, tw, tw, sblk, sblk] + [_const_spec((FFT_N2, FFT_N2))] * 4,
        out_specs=[blk, blk],
        out_shape=[jax.ShapeDtypeStruct((n1, FFT_N2, C), F32)] * 2,
        compiler_params=_cparams(("parallel",)),
        name="hyena_fft_mid",
    )(ar, ai, consts['twr'], consts['twi'], spec[0], spec[1], *consts['f2'])
    depth = bias.shape[0]
    out = pl.pallas_call(
        functools.partial(_fft_inv1_kernel, inv_n=1.0 / (n1 * FFT_N2), passes=passes),
        grid=grid,
        in_specs=[mid, mid] + [_const_spec((n1h, n1))] * 4 + [sig, sig]
                 + [pl.BlockSpec((None, None, 1, cw), lambda i, c: (layer, order, 0, c))],
        out_specs=sig,
        out_shape=jax.ShapeDtypeStruct((2, n1h, FFT_N2, C), out_dtype),
        compiler_params=_cparams(("parallel", "parallel")),
        name="hyena_fft_inv1",
    )(br, bi, *consts['g1'], gate.reshape(2, n1h, FFT_N2, C), yv, bias.reshape(depth, HY_ORDER, 1, C))
    return out.reshape(2, seq_len, C)


def _hy_direct_kernel(y_ref, gate_ref, bias_ref, hr_ref, hi_ref, frh_ref, frl_ref, fih_ref, fil_ref,
                      grh_ref, grl_ref, gih_ref, gil_ref, o_ref, *, inv_n, passes):
    fr, fi = _const_operand(frh_ref, frl_ref, passes), _const_operand(fih_ref, fil_ref, passes)
    sr, si = _cmm(fr, fi, _operand(y_ref[0], passes), _operand(y_ref[1], passes))
    hr, hi = hr_ref[...], hi_ref[...]
    yr, yi = sr * hr - si * hi, sr * hi + si * hr
    gr, gi = _const_operand(grh_ref, grl_ref, passes), _const_operand(gih_ref, gil_ref, passes)
    outr, outi = _cmm(gr, gi, _operand(yr, passes), _operand(yi, passes))
    bias = bias_ref[...]
    o_ref[0] = (gate_ref[0] * (outr * inv_n + y_ref[0] * bias)).astype(o_ref.dtype)
    o_ref[1] = (gate_ref[1] * (outi * inv_n + y_ref[1] * bias)).astype(o_ref.dtype)


def _hy_longconv_direct(y, gate, bias, spec, consts, *, layer, order, seq_len, out_dtype):
    C = y.shape[-1]
    cb = 256
    n = 2 * seq_len
    depth = bias.shape[0]
    data = pl.BlockSpec((2, seq_len, cb), lambda i: (0, 0, i))
    sblk = pl.BlockSpec((None, n, cb), lambda i: (order, 0, i))
    return pl.pallas_call(
        functools.partial(_hy_direct_kernel, inv_n=1.0 / n, passes=FFT_DATA_PASSES),
        grid=(C // cb,),
        in_specs=[data, data, pl.BlockSpec((None, None, 1, cb), lambda i: (layer, order, 0, i)), sblk, sblk]
                 + [_const_spec((n, seq_len))] * 4 + [_const_spec((seq_len, n))] * 4,
        out_specs=data,
        out_shape=jax.ShapeDtypeStruct((2, seq_len, C), out_dtype),
        compiler_params=_cparams(("parallel",)),
        name="hyena_direct_conv",
    )(y, gate, bias.reshape(depth, HY_ORDER, 1, C), spec[0], spec[1], *consts['fwd'], *consts['inv'])


def _hyena(p, filt, hy_short_w, hy_short_b, hy_bias, *, layer, batch, lat_len, ctx_len):
    assert batch == 2
    lat_rows = batch * lat_len
    outs = []
    for seq_len, row0 in ((lat_len, 0), (ctx_len, lat_rows)):
        z = _hy_short(p, hy_short_w, hy_short_b, layer=layer, row0=row0, seq_len=seq_len, batch=batch)
        fft = seq_len >= FFT_MIN_LEN
        consts = _fft_consts(seq_len) if fft else _dft_consts_direct(seq_len)
        kern, scale = filt[seq_len]
        spec = _hyena_spectrum(kern, scale, consts, seq_len=seq_len)
        conv = _hy_longconv_lat if fft else _hy_longconv_direct
        y = z[0]
        for o in range(HY_ORDER):
            y = conv(y, z[o + 1], hy_bias, spec, consts, layer=layer, order=o, seq_len=seq_len,
                     out_dtype=BF16 if o == HY_ORDER - 1 else F32)
        outs.append(y.reshape(batch * seq_len, BRANCH_W))
    return outs


def _merge_kernel(h_ref, mod_ref, ng_ref, y0_ref, y1_ref, y2_ref, y3l_ref, y3c_ref, mw_ref, mb_ref, bw_ref,
                  o_ref, xn_ref, *, lat_tiles):
    i, c, br = pl.program_id(0), pl.program_id(1), pl.program_id(2)

    @pl.when((br == 0) & (c == 0))
    def _():
        xn = _rms(h_ref[...], ng_ref[2:3, :]) * (1.0 + mod_ref[4:5, :]) + mod_ref[3:4, :]
        xn_ref[...] = xn.astype(BF16)

    g = jax.nn.sigmoid(_dot(xn_ref[...], mw_ref[...]) + mb_ref[...])

    @pl.when(br == 0)
    def _():
        o_ref[...] = g * _dot(y0_ref[...], bw_ref[...])

    def accumulate(y_ref):
        o_ref[...] += g * _dot(y_ref[...], bw_ref[...])

    pl.when(br == 1)(lambda: accumulate(y1_ref))
    pl.when(br == 2)(lambda: accumulate(y2_ref))
    pl.when((br == 3) & (i < lat_tiles))(lambda: accumulate(y3l_ref))
    pl.when((br == 3) & (i >= lat_tiles))(lambda: accumulate(y3c_ref))


def _merge(h, mods, norm_gain, ys, merge_w, merge_b, branch_w, tiles):
    d = h.shape[1]
    tm, tn = ROW_TILE, 1024
    nc = d // tn
    l, lat_tiles = tiles.layer, tiles.lat_tiles
    depth = merge_b.shape[0]
    y_ret, y_s5, y_gla, y_hy_lat, y_hy_ctx = ys
    yspec = pl.BlockSpec((tm, BRANCH_W), lambda i, c, br: (i, 0))
    return pl.pallas_call(
        functools.partial(_merge_kernel, lat_tiles=lat_tiles),
        grid=(tiles.n_tiles, nc, N_BRANCH),
        in_specs=[pl.BlockSpec((tm, d), lambda i, c, br: (i, 0)),
                  tiles.mod_spec(d),
                  tiles.gain_spec(d),
                  yspec, yspec, yspec,
                  pl.BlockSpec((tm, BRANCH_W), lambda i, c, br: (jnp.minimum(i, lat_tiles - 1), 0)),
                  pl.BlockSpec((tm, BRANCH_W), lambda i, c, br: (jnp.maximum(i - lat_tiles, 0), 0)),
                  pl.BlockSpec((None, d, tn), lambda i, c, br: (l, 0, br * nc + c)),
                  pl.BlockSpec((None, 1, tn), lambda i, c, br: (l, 0, br * nc + c)),
                  pl.BlockSpec((None, None, BRANCH_W, tn), lambda i, c, br: (l, br, 0, c))],
        out_specs=pl.BlockSpec((tm, tn), lambda i, c, br: (i, c)),
        out_shape=jax.ShapeDtypeStruct((tiles.n_tiles * tm, d), F32),
        scratch_shapes=[pltpu.VMEM((tm, d), BF16)],
        compiler_params=_cparams(("parallel", "arbitrary", "arbitrary")),
        name="merge",
    )(h, mods, norm_gain, y_ret, y_s5, y_gla, y_hy_lat, y_hy_ctx, merge_w, merge_b.reshape(depth, 1, -1), branch_w)


def _outproj_kernel(a_ref, h_ref, mod_ref, ng_ref, w_ref, o_ref):
    y = _dot(a_ref[...].astype(BF16), w_ref[...])
    o_ref[...] = h_ref[...] + mod_ref[5:6, :] * _rms(y, ng_ref[3:4, :])


def _outproj(acc, h, mods, norm_gain, w_out, tiles):
    d = h.shape[1]
    tm = ROW_TILE
    l = tiles.layer
    return pl.pallas_call(
        _outproj_kernel,
        grid=(tiles.n_tiles,),
        in_specs=[pl.BlockSpec((tm, d), lambda i: (i, 0)),
                  pl.BlockSpec((tm, d), lambda i: (i, 0)),
                  tiles.mod_spec(d),
                  tiles.gain_spec(d),
                  pl.BlockSpec((None, d, d), lambda i: (l, 0, 0))],
        out_specs=pl.BlockSpec((tm, d), lambda i: (i, 0)),
        out_shape=jax.ShapeDtypeStruct((tiles.n_tiles * tm, d), F32),
        compiler_params=_cparams(("parallel",)),
        name="outproj",
    )(acc, h, mods, norm_gain, w_out)


def _rope_tables(batch, lat_len, ctx_len):
    rows = lat_len // GRID_W
    row = np.repeat(np.arange(rows, dtype=np.float32), GRID_W)
    col = np.tile(np.arange(GRID_W, dtype=np.float32), rows)
    n_freq = RET_DK // 4
    inv = (ROPE_BASE ** (-np.arange(n_freq, dtype=np.float32) / n_freq)).astype(np.float32)
    ang = np.concatenate([row[:, None] * inv, col[:, None] * inv], axis=-1).astype(np.float32)
    cos, sin = np.cos(ang), np.sin(ang)
    cos2 = np.concatenate([cos, cos], axis=-1)
    sin2 = np.concatenate([-sin, sin], axis=-1)
    ctx_rows = batch * ctx_len
    cos_t = np.concatenate([np.tile(cos2, (batch, 1)), np.ones((ctx_rows, RET_DK), np.float32)], axis=0)
    sin_t = np.concatenate([np.tile(sin2, (batch, 1)), np.zeros((ctx_rows, RET_DK), np.float32)], axis=0)
    return jnp.asarray(cos_t, F32), jnp.asarray(sin_t, F32)


def _reorder_w_in(w):
    pad = jnp.zeros(w.shape[:-1] + (P_COLS - w.shape[-1],), w.dtype)
    lr0 = COL_HY
    lr1 = lr0 + 2 * GLA_RANK
    return jnp.concatenate([w[..., :lr0], w[..., lr1:], w[..., lr0:lr1], pad], axis=-1)


def kernel(x, c, ctx, c_ctx, w_ada, b_ada, norm_gain, ffn_w_gate, ffn_w_up, ffn_w_down, w_in, ret_decay_logit, ret_norm_gain, s5_a_re, s5_a_im, s5_log_dt, s5_b_re, s5_b_im, s5_c_re, s5_c_im, s5_d, s5_glu_w, s5_glu_b, gla_gate_w, gla_gate_b, gla_norm_gain, hy_short_w, hy_short_b, hy_w1, hy_b1, hy_w2, hy_b2, hy_w3, hy_freq, hy_bias, branch_w, merge_w, merge_b, w_out):
    batch, lat_len, d = x.shape
    ctx_len = ctx.shape[1]
    depth = w_ada.shape[0]
    lat_rows = batch * lat_len
    tiles_per_batch = lat_len // ROW_TILE
    lat_tiles = batch * tiles_per_batch
    assert lat_len % ROW_TILE == 0 and (batch * ctx_len) % ROW_TILE == 0
    all_tiles = lat_tiles + batch * ctx_len // ROW_TILE
    seq_kw = dict(batch=batch, lat_len=lat_len, ctx_len=ctx_len)

    cond = jnp.zeros((8, d), F32).at[:batch].set(c).at[batch].set(c_ctx)
    mods = _ada(cond, w_ada, b_ada).reshape(depth, 8, N_MOD, d)
    cos_t, sin_t = _rope_tables(batch, lat_len, ctx_len)

    wg, wu, wd = ffn_w_gate.astype(BF16), ffn_w_up.astype(BF16), ffn_w_down.astype(BF16)
    w_in_b = _reorder_w_in(w_in).astype(BF16)
    merge_w_b, branch_w_b, w_out_b, glu_w_b = (t.astype(BF16) for t in (merge_w, branch_w, w_out, s5_glu_w))
    s5_tab = _s5_tables(s5_a_re, s5_a_im, s5_log_dt, s5_b_re, s5_b_im, s5_c_re, s5_c_im)

    h = jnp.concatenate([x.reshape(lat_rows, d), ctx.reshape(batch * ctx_len, d)], axis=0)
    for l in range(depth):
        full = _Tiles(l, lat_tiles, tiles_per_batch, all_tiles)
        tail = full if l < depth - 1 else _Tiles(l, lat_tiles, tiles_per_batch, lat_tiles)
        h = _ffn(h, mods, norm_gain, wg, wu, wd, full, idx=0)
        p = _inproj(h, mods, norm_gain, w_in_b, full)
        y_ret = _retention(p, cos_t, sin_t, ret_decay_logit, ret_norm_gain, layer=l, **seq_kw)
        y_s5 = _s5(p, s5_tab, s5_d, glu_w_b, s5_glu_b, layer=l, **seq_kw)
        y_gla = _gla(p, gla_gate_w, gla_gate_b, gla_norm_gain, layer=l, **seq_kw)
        filt = {n: _hyena_filter(n, hy_w1[l], hy_b1[l], hy_w2[l], hy_b2[l], hy_w3[l], hy_freq[l])
                for n in (lat_len, ctx_len)}
        y_hy_lat, y_hy_ctx = _hyena(p, filt, hy_short_w, hy_short_b, hy_bias, layer=l, **seq_kw)
        acc = _merge(h, mods, norm_gain, (y_ret, y_s5, y_gla, y_hy_lat, y_hy_ctx), merge_w_b, merge_b,
                     branch_w_b, tail)
        h = _outproj(acc, h, mods, norm_gain, w_out_b, tail)
        h = _ffn(h, mods, norm_gain, wg, wu, wd, tail, idx=1)
    return h.reshape(batch, lat_len, d)
```

```python
import functools
import math

import jax
import jax.numpy as jnp
import numpy as np
from jax import lax
from jax.experimental import pallas as pl
from jax.experimental.pallas import tpu as pltpu

F32 = jnp.float32
BF16 = jnp.bfloat16

D_MODEL = 2048
N_MOD = 9
D_FF = 5632
EPS = 1e-6
GRID_W = 64
BRANCH_W = 1024
N_BRANCH = 4

RET_HEADS, RET_DK, RET_DV, RET_CHUNK = 4, 128, 256, 128
ROPE_BASE = 10000.0
S5_GROUP, S5_GROUPS, S5_STATE = 16, 64, 64
GLA_HEADS, GLA_DK, GLA_DV, GLA_RANK, GLA_TAU, GLA_CHUNK = 4, 128, 256, 16, 16.0, 64
GLA_SUB = 16
RET_STEP_CHUNKS = 2
GLA_STEP_CHUNKS = 4
HY_ORDER, HY_SHORT, HY_BANDS, HY_HIDDEN = 2, 3, 16, 64
HY_TARGET, HY_FAST, HY_SLOW, HY_SHIFT = 1e-2, 0.3, 1.5, 0.05

LANES = 128
ROW_TILE = 512
VMEM_LIMIT = 56 * 1024 * 1024

P_COLS = 10368
COL_RET_Q, COL_RET_K, COL_RET_V, COL_RET_G = 0, 512, 1024, 2048
COL_S5 = 3072
COL_GLA_Q, COL_GLA_K, COL_GLA_V, COL_GLA_G = 4096, 4608, 5120, 6144
COL_HY = 7168
COL_LR = 10240

S5_T = 16
S5_BLK = LANES // S5_GROUP
S5_NBLK = S5_GROUPS // S5_BLK
S5_W = S5_BLK * S5_STATE
S5_LAT_TILE = 4096

FFT_N2 = 128
FFT_MIN_LEN = 2048
FFT_NB = 8
FFT_CW = 512
FFT_DATA_PASSES = 1
FFT_SPEC_PASSES = 1


def _cparams(sem):
    return pltpu.CompilerParams(dimension_semantics=sem, vmem_limit_bytes=VMEM_LIMIT)


def _dot(a, b):
    return jnp.dot(a, b, preferred_element_type=F32)


def _dot_nt(a, b):
    return lax.dot_general(a, b, (((1,), (1,)), ((), ())), preferred_element_type=F32)


def _dot_tn(a, b):
    return lax.dot_general(a, b, (((0,), (0,)), ((), ())), preferred_element_type=F32)


def _split2(x):
    hi = x.astype(BF16)
    lo = (x - hi.astype(F32)).astype(BF16)
    return hi, lo


def _split3(x):
    hi = x.astype(BF16)
    r = x - hi.astype(F32)
    mid = r.astype(BF16)
    lo = (r - mid.astype(F32)).astype(BF16)
    return hi, mid, lo


def _operand(x, passes):
    return _split2(x) if passes == 3 else (x.astype(BF16), None)


def _const_operand(hi_ref, lo_ref, passes):
    return (hi_ref[...], lo_ref[...] if passes == 3 else None)


def _mm(a, b):
    out = _dot(a[0], b[0])
    if a[1] is not None and b[1] is not None:
        out = out + (_dot(a[0], b[1]) + _dot(a[1], b[0]))
    return out


def _cmm(ar, ai, br, bi):
    return _mm(ar, br) - _mm(ai, bi), _mm(ar, bi) + _mm(ai, br)


def _neg(a):
    return (-a[0], None if a[1] is None else -a[1])


def _rms(x, gain):
    return x * lax.rsqrt(jnp.mean(x * x, axis=-1, keepdims=True) + EPS) * gain


def _silu(x):
    return x * jax.nn.sigmoid(x)


def _log_sigmoid(x):
    return jnp.minimum(x, 0.0) - jnp.log1p(jnp.exp(-jnp.abs(x)))


def _mod_index(i, lat_tiles, tiles_per_batch):
    return jnp.where(i < lat_tiles, i // tiles_per_batch, lat_tiles // tiles_per_batch)


def _ada_kernel(c_ref, w_ref, b_ref, o_ref):
    cs = c_ref[...]
    o_ref[...] = _dot(_silu(cs).astype(BF16), w_ref[...].astype(BF16)) + b_ref[...]


def _ada(cond, w_ada, b_ada):
    depth, d, n = w_ada.shape
    tn = 1024
    return pl.pallas_call(
        _ada_kernel,
        grid=(depth, n // tn),
        in_specs=[pl.BlockSpec((8, d), lambda l, j: (0, 0)),
                  pl.BlockSpec((None, d, tn), lambda l, j: (l, 0, j)),
                  pl.BlockSpec((None, 1, tn), lambda l, j: (l, 0, j))],
        out_specs=pl.BlockSpec((None, 8, tn), lambda l, j: (l, 0, j)),
        out_shape=jax.ShapeDtypeStruct((depth, 8, n), F32),
        compiler_params=_cparams(("parallel", "parallel")),
        name="adaln",
    )(cond, w_ada, b_ada.reshape(depth, 1, n))


class _Tiles:
    def __init__(self, layer, lat_tiles, tiles_per_batch, n_tiles):
        self.layer, self.lat_tiles, self.tiles_per_batch, self.n_tiles = layer, lat_tiles, tiles_per_batch, n_tiles

    def mod_spec(self, d):
        l, lt, tpb = self.layer, self.lat_tiles, self.tiles_per_batch
        return pl.BlockSpec((None, None, N_MOD, d), lambda i, *_: (l, _mod_index(i, lt, tpb), 0, 0))

    def gain_spec(self, d):
        l = self.layer
        return pl.BlockSpec((None, 6, d), lambda i, *_: (l, 0, 0))


def _ffn_kernel(h_ref, mod_ref, ng_ref, wg_ref, wu_ref, wd_ref, o_ref, xn_ref, acc_ref, *, mbase, gbase):
    j = pl.program_id(1)

    @pl.when(j == 0)
    def _():
        x = h_ref[...]
        xn = _rms(x, ng_ref[gbase:gbase + 1, :]) * (1.0 + mod_ref[mbase + 1:mbase + 2, :]) + mod_ref[mbase:mbase + 1, :]
        xn_ref[...] = xn.astype(BF16)
        acc_ref[...] = jnp.zeros_like(acc_ref)

    xn = xn_ref[...]
    a = _silu(_dot(xn, wg_ref[...])) * _dot(xn, wu_ref[...])
    acc_ref[...] += _dot(a.astype(BF16), wd_ref[...])

    @pl.when(j == pl.num_programs(1) - 1)
    def _():
        r = _rms(acc_ref[...], ng_ref[gbase + 1:gbase + 2, :])
        o_ref[...] = h_ref[...] + 0.5 * mod_ref[mbase + 2:mbase + 3, :] * r


def _ffn(h, mods, norm_gain, wg, wu, wd, tiles, *, idx):
    d = h.shape[1]
    f = wg.shape[-1]
    tm, tf = ROW_TILE, 512
    l = tiles.layer
    return pl.pallas_call(
        functools.partial(_ffn_kernel, mbase=6 * idx, gbase=4 * idx),
        grid=(tiles.n_tiles, f // tf),
        in_specs=[pl.BlockSpec((tm, d), lambda i, j: (i, 0)),
                  tiles.mod_spec(d),
                  tiles.gain_spec(d),
                  pl.BlockSpec((None, None, d, tf), lambda i, j: (l, idx, 0, j)),
                  pl.BlockSpec((None, None, d, tf), lambda i, j: (l, idx, 0, j)),
                  pl.BlockSpec((None, None, tf, d), lambda i, j: (l, idx, j, 0))],
        out_specs=pl.BlockSpec((tm, d), lambda i, j: (i, 0)),
        out_shape=jax.ShapeDtypeStruct((tiles.n_tiles * tm, d), F32),
        scratch_shapes=[pltpu.VMEM((tm, d), BF16), pltpu.VMEM((tm, d), F32)],
        compiler_params=_cparams(("parallel", "arbitrary")),
        name="ffn",
    )(h, mods, norm_gain, wg, wu, wd)


def _inproj_kernel(h_ref, mod_ref, ng_ref, w_ref, o_ref, xn_ref):
    @pl.when(pl.program_id(1) == 0)
    def _():
        xn = _rms(h_ref[...], ng_ref[2:3, :]) * (1.0 + mod_ref[4:5, :]) + mod_ref[3:4, :]
        xn_ref[...] = xn.astype(BF16)

    o_ref[...] = _dot(xn_ref[...], w_ref[...])


def _inproj(h, mods, norm_gain, w, tiles):
    r, d = h.shape
    n = w.shape[-1]
    tm, tn = ROW_TILE, 1152
    l = tiles.layer
    return pl.pallas_call(
        _inproj_kernel,
        grid=(r // tm, n // tn),
        in_specs=[pl.BlockSpec((tm, d), lambda i, j: (i, 0)),
                  tiles.mod_spec(d),
                  tiles.gain_spec(d),
                  pl.BlockSpec((None, d, tn), lambda i, j: (l, 0, j))],
        out_specs=pl.BlockSpec((tm, tn), lambda i, j: (i, j)),
        out_shape=jax.ShapeDtypeStruct((r, n), F32),
        scratch_shapes=[pltpu.VMEM((tm, d), BF16)],
        compiler_params=_cparams(("parallel", "arbitrary")),
        name="inproj",
    )(h, mods, norm_gain, w)


def _chunk_row_block(b, j, *, lat_chunks, ctx_chunks, reverse, batch):
    ctx0 = batch * lat_chunks + b * ctx_chunks
    lat0 = b * lat_chunks
    jl = j - ctx_chunks
    if reverse:
        return jnp.where(j < ctx_chunks, ctx0 + (ctx_chunks - 1 - j), lat0 + (lat_chunks - 1 - jl))
    return jnp.where(j < ctx_chunks, ctx0 + j, lat0 + jl)


def _ret_kernel(*refs, reverse):
    if reverse:
        dl_ref, q_ref, k_ref, v_ref, cos_ref, sin_ref, of_ref, g_ref, gain_ref, o_ref, s_ref = refs
    else:
        dl_ref, q_ref, k_ref, v_ref, cos_ref, sin_ref, o_ref, s_ref = refs
    C = RET_CHUNK

    @pl.when(pl.program_id(1) == 0)
    def _():
        s_ref[...] = jnp.zeros_like(s_ref)

    ti = lax.broadcasted_iota(jnp.int32, (C, C), 0).astype(F32)
    si = lax.broadcasted_iota(jnp.int32, (C, C), 1).astype(F32)
    tr = lax.broadcasted_iota(jnp.int32, (C, RET_DK), 0).astype(F32)
    rel = (si - ti) if reverse else (ti - si)
    scale = RET_DK ** -0.5
    decays = []
    for h in range(RET_HEADS):
        lg_cc = _log_sigmoid(jnp.full((C, C), dl_ref[h], F32))
        lg_cd = _log_sigmoid(jnp.full((C, RET_DK), dl_ref[h], F32))
        d_intra = jnp.where(rel >= 0, jnp.exp(lg_cc * jnp.maximum(rel, 0.0)), 0.0)
        if reverse:
            d_q = jnp.exp((C - tr) * lg_cd)
            d_k = jnp.exp(tr * lg_cd)
        else:
            d_q = jnp.exp((tr + 1.0) * lg_cd)
            d_k = jnp.exp((C - 1.0 - tr) * lg_cd)
        decays.append((d_intra, d_q, d_k, jnp.exp(C * lg_cd[0:1, 0:1])))

    order = range(RET_STEP_CHUNKS - 1, -1, -1) if reverse else range(RET_STEP_CHUNKS)
    for ci in order:
        rows = slice(ci * C, (ci + 1) * C)
        cos, sin = cos_ref[rows, :], sin_ref[rows, :]
        for h in range(RET_HEADS):
            d_intra, d_q, d_k, d_c = decays[h]
            hk = slice(h * RET_DK, (h + 1) * RET_DK)
            hv = slice(h * RET_DV, (h + 1) * RET_DV)
            qh, kh = q_ref[rows, hk], k_ref[rows, hk]
            vh = v_ref[rows, hv].astype(BF16)
            qr = qh * cos + pltpu.roll(qh, RET_DK // 2, axis=1) * sin
            kr = (kh * cos + pltpu.roll(kh, RET_DK // 2, axis=1) * sin) * scale

            scores = _dot_nt(qr.astype(BF16), kr.astype(BF16)) * d_intra
            s_old = s_ref[h]
            o = _dot(scores.astype(BF16), vh) + _dot((qr * d_q).astype(BF16), s_old.astype(BF16))
            s_ref[h] = s_old * d_c + _dot_tn((kr * d_k).astype(BF16), vh)

            if reverse:
                o = o + of_ref[rows, hv]
                mu = jnp.mean(o, axis=-1, keepdims=True)
                oc = o - mu
                var = jnp.mean(oc * oc, axis=-1, keepdims=True)
                on = oc * lax.rsqrt(var + EPS) * gain_ref[:, hv]
                o = _silu(g_ref[rows, hv]) * on
            o_ref[rows, hv] = o.astype(o_ref.dtype)


def _retention(p, cos_t, sin_t, decay_logit, norm_gain, *, layer, batch, lat_len, ctx_len):
    r = p.shape[0]
    C = RET_CHUNK * RET_STEP_CHUNKS
    lat_chunks, ctx_chunks = lat_len // C, ctx_len // C
    steps = lat_chunks + ctx_chunks

    def call(reverse, extra):
        rb = functools.partial(_chunk_row_block, lat_chunks=lat_chunks, ctx_chunks=ctx_chunks,
                               reverse=reverse, batch=batch)
        row = lambda cb: (lambda b, j: (rb(b, j), cb))
        in_specs = [pl.BlockSpec(memory_space=pltpu.SMEM),
                    pl.BlockSpec((C, 512), row(COL_RET_Q // 512)),
                    pl.BlockSpec((C, 512), row(COL_RET_K // 512)),
                    pl.BlockSpec((C, 1024), row(COL_RET_V // 1024)),
                    pl.BlockSpec((C, RET_DK), row(0)),
                    pl.BlockSpec((C, RET_DK), row(0))]
        args = [decay_logit[layer, 1 if reverse else 0], p, p, p, cos_t, sin_t]
        if reverse:
            in_specs += [pl.BlockSpec((C, 1024), row(0)),
                         pl.BlockSpec((C, 1024), row(COL_RET_G // 1024)),
                         pl.BlockSpec((None, 1, 1024), lambda b, j: (layer, 0, 0))]
            args += [extra, p, norm_gain.reshape(-1, 1, 1024)]
        return pl.pallas_call(
            functools.partial(_ret_kernel, reverse=reverse),
            grid=(batch, steps),
            in_specs=in_specs,
            out_specs=pl.BlockSpec((C, 1024), row(0)),
            out_shape=jax.ShapeDtypeStruct((r, 1024), BF16 if reverse else F32),
            scratch_shapes=[pltpu.VMEM((RET_HEADS, RET_DK, RET_DV), F32)],
            compiler_params=_cparams(("parallel", "arbitrary")),
            name="retention_bwd" if reverse else "retention_fwd",
        )(*args)

    o_f = call(False, None)
    return call(True, o_f)


def _gla_kernel(*refs, reverse):
    if reverse:
        q_ref, k_ref, v_ref, a_ref, gw_ref, gb_ref, of_ref, g_ref, gain_ref, o_ref, s_ref = refs
    else:
        q_ref, k_ref, v_ref, a_ref, gw_ref, gb_ref, o_ref, s_ref = refs
    C, SB = GLA_CHUNK, GLA_SUB

    @pl.when(pl.program_id(1) == 0)
    def _():
        s_ref[...] = jnp.zeros_like(s_ref)

    la_all = _log_sigmoid(_dot(a_ref[...].astype(BF16), gw_ref[...]) + gb_ref[...]) / GLA_TAU
    ti = lax.broadcasted_iota(jnp.int32, (C, C), 0)
    si = lax.broadcasted_iota(jnp.int32, (C, C), 1)
    tri = jnp.where((si >= ti) if reverse else (si <= ti), 1.0, 0.0).astype(BF16)
    ones = jnp.ones((C, LANES), BF16)
    srow = lax.broadcasted_iota(jnp.int32, (C, GLA_DK), 0)
    scale = GLA_DK ** -0.5
    end = 0 if reverse else C - 1

    order = range(GLA_STEP_CHUNKS - 1, -1, -1) if reverse else range(GLA_STEP_CHUNKS)
    for ci in order:
        c0 = ci * C
        rows = slice(c0, c0 + C)
        la3 = _split3(la_all[rows])
        bcum = _dot(tri, la3[0]) + (_dot(tri, la3[1]) + _dot(tri, la3[2]))
        for h in range(GLA_HEADS):
            sl = slice(h * GLA_DK, (h + 1) * GLA_DK)
            cs = slice(h * GLA_DV, (h + 1) * GLA_DV)
            b = bcum[:, sl]
            q = q_ref[rows, sl]
            k = k_ref[rows, sl] * scale
            v = v_ref[rows, cs].astype(BF16)
            s_old = s_ref[h]
            b_end = b[end:end + 1, :]
            o_inter = _dot((q * jnp.exp(b)).astype(BF16), s_old.astype(BF16))
            col = _dot_tn(la3[0][:, sl], ones) + (_dot_tn(la3[1][:, sl], ones) + _dot_tn(la3[2][:, sl], ones))
            dec_col = jnp.exp(jnp.concatenate([col, col], axis=1))
            s_ref[h] = s_old * dec_col + _dot_tn((k * jnp.exp(b_end - b)).astype(BF16), v)

            for blk in range(C // SB):
                r0 = blk * SB
                ref_row = b[r0 + SB - 1:r0 + SB, :] if reverse else b[r0:r0 + 1, :]
                reach = (srow >= r0) if reverse else (srow < r0 + SB)
                qt = (q[r0:r0 + SB] * jnp.exp(b[r0:r0 + SB] - ref_row)).astype(BF16)
                kt = (k * jnp.exp(jnp.where(reach, ref_row - b, 0.0))).astype(BF16)
                sc = _dot_nt(qt, kt)
                tt = lax.broadcasted_iota(jnp.int32, sc.shape, 0) + r0
                ss = lax.broadcasted_iota(jnp.int32, sc.shape, 1)
                sc = jnp.where((ss >= tt) if reverse else (ss <= tt), sc, 0.0)
                o = _dot(sc.astype(BF16), v) + o_inter[r0:r0 + SB]
                orow = slice(c0 + r0, c0 + r0 + SB)
                if reverse:
                    o = o + of_ref[orow, cs]
                    on = o * lax.rsqrt(jnp.mean(o * o, axis=-1, keepdims=True) + EPS) * gain_ref[:, cs]
                    o = _silu(g_ref[orow, cs]) * on
                o_ref[orow, cs] = o.astype(o_ref.dtype)


def _gla(p, gate_w, gate_b, norm_gain, *, layer, batch, lat_len, ctx_len):
    r = p.shape[0]
    C = GLA_CHUNK * GLA_STEP_CHUNKS
    lat_chunks, ctx_chunks = lat_len // C, ctx_len // C
    steps = lat_chunks + ctx_chunks
    hk = GLA_HEADS * GLA_DK
    depth = gate_w.shape[0]
    gw = jnp.zeros((depth, 2, LANES, hk), F32)
    for d in range(2):
        gw = gw.at[:, d, d * GLA_RANK:(d + 1) * GLA_RANK].set(gate_w[:, d])
    gw = gw.astype(BF16)

    def call(reverse, extra):
        d = 1 if reverse else 0
        rb = functools.partial(_chunk_row_block, lat_chunks=lat_chunks, ctx_chunks=ctx_chunks,
                               reverse=reverse, batch=batch)
        row = lambda cb: (lambda b, j: (rb(b, j), cb))
        in_specs = [pl.BlockSpec((C, 512), row(COL_GLA_Q // 512)),
                    pl.BlockSpec((C, 512), row(COL_GLA_K // 512)),
                    pl.BlockSpec((C, 1024), row(COL_GLA_V // 1024)),
                    pl.BlockSpec((C, LANES), row(COL_LR // LANES)),
                    pl.BlockSpec((None, None, LANES, hk), lambda b, j: (layer, d, 0, 0)),
                    pl.BlockSpec((None, None, 1, hk), lambda b, j: (layer, d, 0, 0))]
        args = [p, p, p, p, gw, gate_b.reshape(depth, 2, 1, hk)]
        if reverse:
            in_specs += [pl.BlockSpec((C, 1024), row(0)),
                         pl.BlockSpec((C, 1024), row(COL_GLA_G // 1024)),
                         pl.BlockSpec((None, 1, 1024), lambda b, j: (layer, 0, 0))]
            args += [extra, p, norm_gain.reshape(depth, 1, 1024)]
        return pl.pallas_call(
            functools.partial(_gla_kernel, reverse=reverse),
            grid=(batch, steps),
            in_specs=in_specs,
            out_specs=pl.BlockSpec((C, 1024), row(0)),
            out_shape=jax.ShapeDtypeStruct((r, 1024), BF16 if reverse else F32),
            scratch_shapes=[pltpu.VMEM((GLA_HEADS, GLA_DK, GLA_DV), F32)],
            compiler_params=_cparams(("parallel", "arbitrary")),
            name="gla_bwd" if reverse else "gla_fwd",
        )(*args)

    o_f = call(False, None)
    return call(True, o_f)


def _s5_tables(a_re, a_im, log_dt, b_re, b_im, c_re, c_im):
    T, P, K = S5_T, S5_STATE, S5_GROUP
    lead = a_re.shape[:-2]
    nl = len(lead)
    dt = jnp.exp(log_dt)[..., None]
    mag = jnp.exp(a_re * dt)
    ab_re, ab_im = mag * jnp.cos(a_im * dt), mag * jnp.sin(a_im * dt)
    den = a_re * a_re + a_im * a_im
    nr = ab_re - 1.0
    f_re = (nr * a_re + ab_im * a_im) / den
    f_im = (ab_im * a_re - nr * a_im) / den
    bb_re = f_re[..., None] * b_re - f_im[..., None] * b_im
    bb_im = f_re[..., None] * b_im + f_im[..., None] * b_re

    def power(n):
        n = jnp.asarray(n, F32)[:, None, None]
        m = jnp.exp(n * (a_re * dt)[..., None, :, :])
        ang = n * (a_im * dt)[..., None, :, :]
        return m * jnp.cos(ang), m * jnp.sin(ang)

    eye = jnp.eye(S5_BLK, dtype=F32)

    def blockdiag_in(x):
        x = x.reshape(*lead, T, S5_NBLK, S5_BLK, P, K)
        x = jnp.moveaxis(x, nl, nl + 1)
        x = jnp.swapaxes(x, -1, -2)
        y = x[..., :, :, None, :] * eye[:, None, :, None]
        return y.reshape(*lead, S5_NBLK, T, LANES, S5_W)

    pr, pi = power(np.arange(T - 1, -1, -1))
    bre, bim = bb_re[..., None, :, :, :], bb_im[..., None, :, :, :]
    w_re = pr[..., None] * bre - pi[..., None] * bim
    w_im = pr[..., None] * bim + pi[..., None] * bre
    w_end = jnp.concatenate([blockdiag_in(w_re), blockdiag_in(w_im)], axis=-1).astype(BF16)

    qr, qi = power(np.arange(T))
    lb_re = qr[..., None] * bre - qi[..., None] * bim
    lb_im = qr[..., None] * bim + qi[..., None] * bre
    kj = (jnp.einsum('...gop,...jgpi->...jgio', c_re, lb_re)
          - jnp.einsum('...gop,...jgpi->...jgio', c_im, lb_im))
    kj = kj.reshape(*lead, T, S5_NBLK, S5_BLK, K, K)
    kj = jnp.moveaxis(kj, nl, nl + 1)
    kj = kj[..., :, :, None, :] * eye[:, None, :, None]
    kj = kj.reshape(*lead, S5_NBLK, T, LANES, LANES).astype(BF16)

    def blockdiag_out(x):
        x = jnp.swapaxes(x.reshape(*lead, S5_NBLK, S5_BLK, K, P), -1, -2)
        y = x[..., :, :, None, :] * eye[:, None, :, None]
        return y.reshape(*lead, S5_NBLK, S5_W, LANES)

    c_out = jnp.concatenate([blockdiag_out(c_re), -blockdiag_out(c_im)], axis=-2).astype(BF16)

    sr, si = power(np.arange(1, T + 1))
    lam_pow = jnp.concatenate([sr.reshape(*lead, T, S5_NBLK, 1, S5_W), si.reshape(*lead, T, S5_NBLK, 1, S5_W)], axis=-1)
    lam_pow = jnp.moveaxis(lam_pow, nl, nl + 1)
    return w_end, kj, c_out, lam_pow


def _s5_kernel(u_ref, x0_ref, wend_ref, kj_ref, cout_ref, lam_ref, y_ref, xf_ref,
               ubuf_ref, sloc_ref, xin_ref, x_ref, *, reverse, n):
    T, W = S5_T, S5_W

    @pl.when(pl.program_id(2) == 0)
    def _():
        x_ref[...] = x0_ref[...]

    off = lambda r: (T - 1 - r) if reverse else r
    for r in range(T):
        ubuf_ref[r] = u_ref[pl.ds(off(r), n, stride=T), :].astype(BF16)

    sloc = _dot(ubuf_ref[0], wend_ref[0])
    for r in range(1, T):
        sloc = sloc + _dot(ubuf_ref[r], wend_ref[r])
    sloc_ref[...] = sloc

    lt = lam_ref[T - 1]
    ltr, lti = lt[:, :W], lt[:, W:]

    def body(i, x):
        c = (n - 1 - i) if reverse else i
        xin_ref[pl.ds(c, 1), :] = x
        s = sloc_ref[pl.ds(c, 1), :]
        xr, xi = x[:, :W], x[:, W:]
        return jnp.concatenate([ltr * xr - lti * xi + s[:, :W], ltr * xi + lti * xr + s[:, W:]], axis=1)

    x_fin = lax.fori_loop(0, n, body, x_ref[...])
    x_ref[...] = x_fin
    xf_ref[...] = x_fin

    xin = xin_ref[...]
    xr, xi = xin[:, :W], xin[:, W:]
    cout = cout_ref[...]
    for r in range(T):
        lp = lam_ref[r]
        pr, pi = lp[:, :W], lp[:, W:]
        z = jnp.concatenate([pr * xr - pi * xi, pr * xi + pi * xr], axis=1)
        y = _dot(z.astype(BF16), cout)
        for j in range(r + 1):
            y = y + _dot(ubuf_ref[r - j], kj_ref[j])
        y_ref[pl.ds(off(r), n, stride=T), :] = y


def _s5_scan(p, tables, x0, *, layer, reverse, row0, rows_per_batch, tile, batch):
    w_end, kj, c_out, lam_pow = tables
    T, W = S5_T, S5_W
    n = tile // T
    steps = rows_per_batch // tile
    blk0 = row0 // tile
    d = 1 if reverse else 0

    def urow(kb, b, j):
        jj = (steps - 1 - j) if reverse else j
        return blk0 + b * steps + jj

    tab = lambda *tail: pl.BlockSpec((None, None, None) + tail, lambda kb, b, j: (layer, d, kb) + (0,) * len(tail))
    y, xf = pl.pallas_call(
        functools.partial(_s5_kernel, reverse=reverse, n=n),
        grid=(S5_NBLK, batch, steps),
        in_specs=[pl.BlockSpec((tile, LANES), lambda kb, b, j: (urow(kb, b, j), COL_S5 // LANES + kb)),
                  pl.BlockSpec((None, 1, 2 * W), lambda kb, b, j: (b * S5_NBLK + kb, 0, 0)),
                  tab(T, LANES, 2 * W), tab(T, LANES, LANES), tab(2 * W, LANES), tab(T, 1, 2 * W)],
        out_specs=[pl.BlockSpec((tile, LANES), lambda kb, b, j: (urow(kb, b, j) - blk0, kb)),
                   pl.BlockSpec((None, 1, 2 * W), lambda kb, b, j: (b * S5_NBLK + kb, 0, 0))],
        out_shape=[jax.ShapeDtypeStruct((batch * rows_per_batch, BRANCH_W), F32),
                   jax.ShapeDtypeStruct((batch * S5_NBLK, 1, 2 * W), F32)],
        scratch_shapes=[pltpu.VMEM((T, n, LANES), BF16), pltpu.VMEM((n, 2 * W), F32),
                        pltpu.VMEM((n, 2 * W), F32), pltpu.VMEM((1, 2 * W), F32)],
        compiler_params=_cparams(("parallel", "parallel", "arbitrary")),
        name="s5_scan",
    )(p, x0, w_end, kj, c_out, lam_pow)
    return y, xf


def _gelu_tanh(x):
    return 0.5 * x * (1.0 + jnp.tanh(math.sqrt(2.0 / math.pi) * (x + 0.044715 * (x * x * x))))


def _s5_out_kernel(ylf_ref, ylb_ref, ycf_ref, ycb_ref, u_ref, d_ref, w_ref, b_ref, o_ref, *, lat_tiles):
    def emit(yf_ref, yb_ref):
        y = _gelu_tanh(yf_ref[...] + yb_ref[...] + d_ref[...] * u_ref[...])
        o_ref[...] = (y * jax.nn.sigmoid(_dot(y.astype(BF16), w_ref[...]) + b_ref[...])).astype(o_ref.dtype)

    i = pl.program_id(0)
    pl.when(i < lat_tiles)(lambda: emit(ylf_ref, ylb_ref))
    pl.when(i >= lat_tiles)(lambda: emit(ycf_ref, ycb_ref))


def _s5(p, tables, s5_d, glu_w, glu_b, *, layer, batch, lat_len, ctx_len):
    r = p.shape[0]
    lat_rows = batch * lat_len
    zeros = jnp.zeros((batch * S5_NBLK, 1, 2 * S5_W), F32)
    ys = []
    for d in range(2):
        kw = dict(layer=layer, reverse=bool(d), batch=batch)
        y_c, x_c = _s5_scan(p, tables, zeros, row0=lat_rows, rows_per_batch=ctx_len, tile=ctx_len, **kw)
        y_l, _ = _s5_scan(p, tables, x_c, row0=0, rows_per_batch=lat_len, tile=min(lat_len, S5_LAT_TILE), **kw)
        ys += [y_l, y_c]
    tm = ROW_TILE
    lat_tiles = lat_rows // tm
    depth = s5_d.shape[0]
    lat = pl.BlockSpec((tm, BRANCH_W), lambda i: (jnp.minimum(i, lat_tiles - 1), 0))
    ctx = pl.BlockSpec((tm, BRANCH_W), lambda i: (jnp.maximum(i - lat_tiles, 0), 0))
    return pl.pallas_call(
        functools.partial(_s5_out_kernel, lat_tiles=lat_tiles),
        grid=(r // tm,),
        in_specs=[lat, lat, ctx, ctx,
                  pl.BlockSpec((tm, BRANCH_W), lambda i: (i, COL_S5 // BRANCH_W)),
                  pl.BlockSpec((None, 1, BRANCH_W), lambda i: (layer, 0, 0)),
                  pl.BlockSpec((None, BRANCH_W, BRANCH_W), lambda i: (layer, 0, 0)),
                  pl.BlockSpec((None, 1, BRANCH_W), lambda i: (layer, 0, 0))],
        out_specs=pl.BlockSpec((tm, BRANCH_W), lambda i: (i, 0)),
        out_shape=jax.ShapeDtypeStruct((r, BRANCH_W), BF16),
        compiler_params=_cparams(("parallel",)),
        name="s5_out",
    )(ys[0], ys[2], ys[1], ys[3], p, s5_d.reshape(depth, 1, -1), glu_w, glu_b.reshape(depth, 1, -1))


def _hy_short_kernel(z_ref, w_ref, b_ref, o_ref):
    z = z_ref[...]
    n = z.shape[0]
    t = lax.broadcasted_iota(jnp.int32, z.shape, 0)
    zm = jnp.where(t == 0, 0.0, pltpu.roll(z, 1, axis=0))
    zp = jnp.where(t == n - 1, 0.0, pltpu.roll(z, n - 1, axis=0))
    o_ref[...] = w_ref[0:1, :] * zm + w_ref[1:2, :] * z + w_ref[2:3, :] * zp + b_ref[...]


def _hy_short(p, w, b, *, layer, row0, seq_len, batch):
    cb = 256
    nb = BRANCH_W // cb
    blk0 = row0 // seq_len
    depth = w.shape[0]
    return pl.pallas_call(
        _hy_short_kernel,
        grid=(HY_ORDER + 1, batch, nb),
        in_specs=[pl.BlockSpec((seq_len, cb), lambda s, bb, c: (blk0 + bb, COL_HY // cb + s * nb + c)),
                  pl.BlockSpec((None, HY_SHORT, cb), lambda s, bb, c: (layer, 0, s * nb + c)),
                  pl.BlockSpec((None, 1, cb), lambda s, bb, c: (layer, 0, s * nb + c))],
        out_specs=pl.BlockSpec((None, None, seq_len, cb), lambda s, bb, c: (s, bb, 0, c)),
        out_shape=jax.ShapeDtypeStruct((HY_ORDER + 1, batch, seq_len, BRANCH_W), F32),
        compiler_params=_cparams(("parallel", "parallel", "parallel")),
        name="hyena_short",
    )(p, w, b.reshape(depth, 1, -1))


def _dft(n_out, n_in, n, sign):
    k = np.arange(n_out)[:, None].astype(np.float64)
    m = np.arange(n_in)[None, :].astype(np.float64)
    ang = sign * 2.0 * np.pi * ((k * m) % n) / n
    return np.cos(ang), np.sin(ang)


def _const_split(x):
    return _split2(jnp.asarray(x, F32))


def _const_spec(shape):
    return pl.BlockSpec(shape, lambda *_: (0,) * len(shape))


def _fft_consts(seq_len):
    n = 2 * seq_len
    n1 = n // FFT_N2
    n1h = seq_len // FFT_N2
    eye = np.eye(FFT_NB)
    c1, s1 = _dft(n1, n1, n1, -1.0)
    f1_full = _const_split(np.kron(c1, eye)) + _const_split(np.kron(s1, eye))
    f1 = _const_split(np.kron(c1[:, :n1h], eye)) + _const_split(np.kron(s1[:, :n1h], eye))
    c2, s2 = _dft(FFT_N2, FFT_N2, FFT_N2, -1.0)
    f2 = _const_split(c2) + _const_split(s2)
    cg, sg = _dft(n1h, n1, n1, 1.0)
    g1 = _const_split(np.kron(cg, eye)) + _const_split(np.kron(sg, eye))
    k1 = np.arange(n1)[:, None].astype(np.float64)
    n2 = np.arange(FFT_N2)[None, :].astype(np.float64)
    ang = -2.0 * np.pi * (k1 * n2) / n
    twr = jnp.asarray(np.repeat(np.cos(ang)[:, :, None], LANES, axis=2), F32)
    twi = jnp.asarray(np.repeat(np.sin(ang)[:, :, None], LANES, axis=2), F32)
    return dict(f1_full=f1_full, f1=f1, f2=f2, g1=g1, twr=twr, twi=twi)


def _dft_consts_direct(seq_len):
    n = 2 * seq_len
    c, s = _dft(n, n, n, -1.0)
    full = _const_split(c) + _const_split(s)
    fwd = _const_split(c[:, :seq_len]) + _const_split(s[:, :seq_len])
    c, s = _dft(seq_len, n, n, 1.0)
    inv = _const_split(c) + _const_split(s)
    return dict(full=full, fwd=fwd, inv=inv)


def _hyena_filter(L, w1, b1, w2, b2, w3, freq):
    pos = np.concatenate([np.arange(L), [0], np.arange(L - 1, 0, -1)])
    fwd = jnp.asarray((np.arange(2 * L) < L)[:, None, None], F32)
    bwd = jnp.asarray((np.arange(2 * L) > L)[:, None, None], F32)
    t = jnp.linspace(0.0, 1.0, L, dtype=F32)[pos]
    w = (2.0 * math.pi * jnp.arange(L, dtype=F32) / L)[pos]
    bands = jnp.linspace(1e-4, HY_BANDS - 1.0, HY_BANDS, dtype=F32)
    ph = w[:, None] * bands[None, :]
    z = jnp.concatenate([t[:, None], jnp.cos(ph), -jnp.sin(ph)], axis=-1)
    h = jnp.sin(freq[0] * (z @ w1 + b1))
    h = jnp.sin(freq[1] * (h @ w2 + b2))
    h = (h @ w3).reshape(2 * L, HY_ORDER, 2, BRANCH_W)
    deltas = jnp.abs(jnp.linspace(math.log(HY_TARGET) / HY_FAST, math.log(HY_TARGET) / HY_SLOW, BRANCH_W, dtype=F32))
    window = jnp.exp(-t[:, None] * deltas[None, :]) + HY_SHIFT
    kern = (h[:, :, 0] * fwd + h[:, :, 1] * bwd) * window[:, None, :]
    scale = lax.rsqrt(jnp.sum(kern * kern, axis=0) + EPS)
    return kern.reshape(2 * L, HY_ORDER * BRANCH_W), scale.reshape(HY_ORDER, 1, BRANCH_W)


def _flat(x):
    return x.reshape(x.shape[0] * x.shape[1], x.shape[2])


def _spec_fwd1_kernel(z_ref, frh_ref, frl_ref, fih_ref, fil_ref, ar_ref, ai_ref):
    p = FFT_SPEC_PASSES
    z = _operand(_flat(z_ref[...]), p)
    ar_ref[...] = _mm(_const_operand(frh_ref, frl_ref, p), z).reshape(ar_ref.shape)
    ai_ref[...] = _mm(_const_operand(fih_ref, fil_ref, p), z).reshape(ai_ref.shape)


def _spec_fwd2_kernel(ar_ref, ai_ref, twr_ref, twi_ref, frh_ref, frl_ref, fih_ref, fil_ref, sc_ref,
                      hr_ref, hi_ref, *, reps):
    p = FFT_SPEC_PASSES
    twr = jnp.concatenate([twr_ref[...]] * reps, axis=1)
    twi = jnp.concatenate([twi_ref[...]] * reps, axis=1)
    ar, ai = ar_ref[...], ai_ref[...]
    xr, xi = ar * twr - ai * twi, ar * twi + ai * twr
    sr, si = _cmm(_const_operand(frh_ref, frl_ref, p), _const_operand(fih_ref, fil_ref, p),
                  _operand(xr, p), _operand(xi, p))
    hr_ref[...] = sr * sc_ref[...]
    hi_ref[...] = si * sc_ref[...]


def _spec_direct_kernel(z_ref, frh_ref, frl_ref, fih_ref, fil_ref, sc_ref, hr_ref, hi_ref):
    p = FFT_SPEC_PASSES
    z = _operand(z_ref[...], p)
    hr_ref[...] = _mm(_const_operand(frh_ref, frl_ref, p), z) * sc_ref[...]
    hi_ref[...] = _mm(_const_operand(fih_ref, fil_ref, p), z) * sc_ref[...]


def _hyena_spectrum(kern, scale, consts, *, seq_len):
    C = BRANCH_W
    n = 2 * seq_len
    if seq_len < FFT_MIN_LEN:
        cb = 256
        nb = C // cb
        return pl.pallas_call(
            _spec_direct_kernel,
            grid=(HY_ORDER, nb),
            in_specs=[pl.BlockSpec((n, cb), lambda o, i: (0, o * nb + i))] + [_const_spec((n, n))] * 4
                     + [pl.BlockSpec((None, 1, cb), lambda o, i: (o, 0, i))],
            out_specs=[pl.BlockSpec((None, n, cb), lambda o, i: (o, 0, i))] * 2,
            out_shape=[jax.ShapeDtypeStruct((HY_ORDER, n, C), F32)] * 2,
            compiler_params=_cparams(("parallel", "parallel")),
            name="hyena_spec_direct",
        )(kern, *consts['full'], scale)
    n1 = n // FFT_N2
    cw = FFT_CW
    nc = C // cw
    ar, ai = pl.pallas_call(
        _spec_fwd1_kernel,
        grid=(HY_ORDER, FFT_N2 // FFT_NB, nc),
        in_specs=[pl.BlockSpec((n1, FFT_NB, cw), lambda o, i, c: (0, i, o * nc + c))]
                 + [_const_spec((n1 * FFT_NB, n1 * FFT_NB))] * 4,
        out_specs=[pl.BlockSpec((None, n1, FFT_NB, cw), lambda o, i, c: (o, 0, i, c))] * 2,
        out_shape=[jax.ShapeDtypeStruct((HY_ORDER, n1, FFT_N2, C), F32)] * 2,
        compiler_params=_cparams(("parallel", "parallel", "parallel")),
        name="hyena_spec_fwd1",
    )(kern.reshape(n1, FFT_N2, HY_ORDER * C), *consts['f1_full'])
    blk = pl.BlockSpec((None, None, FFT_N2, C), lambda o, i: (o, i, 0, 0))
    tw = pl.BlockSpec((None, FFT_N2, LANES), lambda o, i: (i, 0, 0))
    return pl.pallas_call(
        functools.partial(_spec_fwd2_kernel, reps=C // LANES),
        grid=(HY_ORDER, n1),
        in_specs=[blk, blk, tw, tw] + [_const_spec((FFT_N2, FFT_N2))] * 4
                 + [pl.BlockSpec((None, 1, C), lambda o, i: (o, 0, 0))],
        out_specs=[blk, blk],
        out_shape=[jax.ShapeDtypeStruct((HY_ORDER, n1, FFT_N2, C), F32)] * 2,
        compiler_params=_cparams(("parallel", "parallel")),
        name="hyena_spec_fwd2",
    )(ar, ai, consts['twr'], consts['twi'], *consts['f2'], scale)


def _fft_fwd1_kernel(z_ref, frh_ref, frl_ref, fih_ref, fil_ref, ar_ref, ai_ref, *, passes):
    fr, fi = _const_operand(frh_ref, frl_ref, passes), _const_operand(fih_ref, fil_ref, passes)
    ar, ai = _cmm(fr, fi, _operand(_flat(z_ref[0]), passes), _operand(_flat(z_ref[1]), passes))
    ar_ref[...] = ar.reshape(ar_ref.shape)
    ai_ref[...] = ai.reshape(ai_ref.shape)


def _fft_mid_kernel(ar_ref, ai_ref, twr_ref, twi_ref, hr_ref, hi_ref, frh_ref, frl_ref, fih_ref, fil_ref,
                    br_ref, bi_ref, *, reps, passes):
    twr = jnp.concatenate([twr_ref[...]] * reps, axis=1)
    twi = jnp.concatenate([twi_ref[...]] * reps, axis=1)
    ar, ai = ar_ref[...], ai_ref[...]
    xr, xi = ar * twr - ai * twi, ar * twi + ai * twr
    fr, fi = _const_operand(frh_ref, frl_ref, passes), _const_operand(fih_ref, fil_ref, passes)
    sr, si = _cmm(fr, fi, _operand(xr, passes), _operand(xi, passes))
    hr, hi = hr_ref[...], hi_ref[...]
    yr, yi = sr * hr - si * hi, sr * hi + si * hr
    br, bi = _cmm(fr, _neg(fi), _operand(yr, passes), _operand(yi, passes))
    br_ref[...] = br * twr + bi * twi
    bi_ref[...] = bi * twr - br * twi


def _fft_inv1_kernel(br_ref, bi_ref, grh_ref, grl_ref, gih_ref, gil_ref, gate_ref, y_ref, bias_ref, o_ref, *,
                     inv_n, passes):
    gr, gi = _const_operand(grh_ref, grl_ref, passes), _const_operand(gih_ref, gil_ref, passes)
    out = _cmm(gr, gi, _operand(_flat(br_ref[...]), passes), _operand(_flat(bi_ref[...]), passes))
    bias = bias_ref[...]
    for b in range(2):
        conv = out[b].reshape(o_ref.shape[1:]) * inv_n
        o_ref[b] = (gate_ref[b] * (conv + y_ref[b] * bias)).astype(o_ref.dtype)


def _hy_longconv_lat(y, gate, bias, spec, consts, *, layer, order, seq_len, out_dtype):
    n1 = 2 * seq_len // FFT_N2
    n1h = seq_len // FFT_N2
    (y_arr, y_part), (g_arr, g_part) = y, gate
    C = y_arr.shape[-1]
    passes = FFT_DATA_PASSES
    cw = FFT_CW
    grid = (FFT_N2 // FFT_NB, C // cw)
    sig = lambda part: pl.BlockSpec((None, 2, n1h, FFT_NB, cw), lambda i, c: (part, 0, 0, i, c))
    mid = pl.BlockSpec((n1, FFT_NB, cw), lambda i, c: (0, i, c))
    ar, ai = pl.pallas_call(
        functools.partial(_fft_fwd1_kernel, passes=passes),
        grid=grid,
        in_specs=[sig(y_part)] + [_const_spec((n1 * FFT_NB, n1h * FFT_NB))] * 4,
        out_specs=[mid, mid],
        out_shape=[jax.ShapeDtypeStruct((n1, FFT_N2, C), F32)] * 2,
        compiler_params=_cparams(("parallel", "parallel")),
        name="hyena_fft_fwd1",
    )(y_arr, *consts['f1'])
    blk = pl.BlockSpec((None, FFT_N2, C), lambda i: (i, 0, 0))
    sblk = pl.BlockSpec((None, None, FFT_N2, C), lambda i: (order, i, 0, 0))
    tw = pl.BlockSpec((None, FFT_N2, LANES), lambda i: (i, 0, 0))
    br, bi = pl.pallas_call(
        functools.partial(_fft_mid_kernel, reps=C // LANES, passes=passes),
        grid=(n1,),
        in_specs=[blk, blk, tw, tw, sblk, sblk] + [_const_spec((FFT_N2, FFT_N2))] * 4,
        out_specs=[blk, blk],
        out_shape=[jax.ShapeDtypeStruct((n1, FFT_N2, C), F32)] * 2,
        compiler_params=_cparams(("parallel",)),
        name="hyena_fft_mid",
    )(ar, ai, consts['twr'], consts['twi'], spec[0], spec[1], *consts['f2'])
    depth = bias.shape[0]
    return pl.pallas_call(
        functools.partial(_fft_inv1_kernel, inv_n=1.0 / (n1 * FFT_N2), passes=passes),
        grid=grid,
        in_specs=[mid, mid] + [_const_spec((n1h * FFT_NB, n1 * FFT_NB))] * 4 + [sig(g_part), sig(y_part)]
                 + [pl.BlockSpec((None, None, 1, cw), lambda i, c: (layer, order, 0, c))],
        out_specs=sig(0),
        out_shape=jax.ShapeDtypeStruct((1, 2, n1h, FFT_N2, C), out_dtype),
        compiler_params=_cparams(("parallel", "parallel")),
        name="hyena_fft_inv1",
    )(br, bi, *consts['g1'], g_arr, y_arr, bias.reshape(depth, HY_ORDER, 1, C))


def _hy_direct_kernel(y_ref, gate_ref, bias_ref, hr_ref, hi_ref, frh_ref, frl_ref, fih_ref, fil_ref,
                      grh_ref, grl_ref, gih_ref, gil_ref, o_ref, *, inv_n, passes):
    fr, fi = _const_operand(frh_ref, frl_ref, passes), _const_operand(fih_ref, fil_ref, passes)
    sr, si = _cmm(fr, fi, _operand(y_ref[0], passes), _operand(y_ref[1], passes))
    hr, hi = hr_ref[...], hi_ref[...]
    yr, yi = sr * hr - si * hi, sr * hi + si * hr
    gr, gi = _const_operand(grh_ref, grl_ref, passes), _const_operand(gih_ref, gil_ref, passes)
    outr, outi = _cmm(gr, gi, _operand(yr, passes), _operand(yi, passes))
    bias = bias_ref[...]
    o_ref[0] = (gate_ref[0] * (outr * inv_n + y_ref[0] * bias)).astype(o_ref.dtype)
    o_ref[1] = (gate_ref[1] * (outi * inv_n + y_ref[1] * bias)).astype(o_ref.dtype)


def _hy_longconv_direct(y, gate, bias, spec, consts, *, layer, order, seq_len, out_dtype):
    C = y.shape[-1]
    cb = 256
    n = 2 * seq_len
    depth = bias.shape[0]
    data = pl.BlockSpec((2, seq_len, cb), lambda i: (0, 0, i))
    sblk = pl.BlockSpec((None, n, cb), lambda i: (order, 0, i))
    return pl.pallas_call(
        functools.partial(_hy_direct_kernel, inv_n=1.0 / n, passes=FFT_DATA_PASSES),
        grid=(C // cb,),
        in_specs=[data, data, pl.BlockSpec((None, None, 1, cb), lambda i: (layer, order, 0, i)), sblk, sblk]
                 + [_const_spec((n, seq_len))] * 4 + [_const_spec((seq_len, n))] * 4,
        out_specs=data,
        out_shape=jax.ShapeDtypeStruct((2, seq_len, C), out_dtype),
        compiler_params=_cparams(("parallel",)),
        name="hyena_direct_conv",
    )(y, gate, bias.reshape(depth, HY_ORDER, 1, C), spec[0], spec[1], *consts['fwd'], *consts['inv'])


def _hyena(p, filt, hy_short_w, hy_short_b, hy_bias, *, layer, batch, lat_len, ctx_len):
    assert batch == 2
    lat_rows = batch * lat_len
    outs = []
    for seq_len, row0 in ((lat_len, 0), (ctx_len, lat_rows)):
        z = _hy_short(p, hy_short_w, hy_short_b, layer=layer, row0=row0, seq_len=seq_len, batch=batch)
        fft = seq_len >= FFT_MIN_LEN
        consts = _fft_consts(seq_len) if fft else _dft_consts_direct(seq_len)
        kern, scale = filt[seq_len]
        spec = _hyena_spectrum(kern, scale, consts, seq_len=seq_len)
        kw = dict(layer=layer, seq_len=seq_len)
        if fft:
            z5 = z.reshape(HY_ORDER + 1, batch, seq_len // FFT_N2, FFT_N2, BRANCH_W)
            y = (z5, 0)
            for o in range(HY_ORDER):
                y = (_hy_longconv_lat(y, (z5, o + 1), hy_bias, spec, consts, order=o,
                                      out_dtype=BF16 if o == HY_ORDER - 1 else F32, **kw), 0)
            y = y[0]
        else:
            y = z[0]
            for o in range(HY_ORDER):
                y = _hy_longconv_direct(y, z[o + 1], hy_bias, spec, consts, order=o,
                                        out_dtype=BF16 if o == HY_ORDER - 1 else F32, **kw)
        outs.append(y.reshape(batch * seq_len, BRANCH_W))
    return outs


def _merge_kernel(h_ref, mod_ref, ng_ref, y0_ref, y1_ref, y2_ref, y3l_ref, y3c_ref, mw_ref, mb_ref, bw_ref,
                  o_ref, xn_ref, *, lat_tiles):
    i, c, br = pl.program_id(0), pl.program_id(1), pl.program_id(2)

    @pl.when((br == 0) & (c == 0))
    def _():
        xn = _rms(h_ref[...], ng_ref[2:3, :]) * (1.0 + mod_ref[4:5, :]) + mod_ref[3:4, :]
        xn_ref[...] = xn.astype(BF16)

    g = jax.nn.sigmoid(_dot(xn_ref[...], mw_ref[...]) + mb_ref[...])

    @pl.when(br == 0)
    def _():
        o_ref[...] = g * _dot(y0_ref[...], bw_ref[...])

    def accumulate(y_ref):
        o_ref[...] += g * _dot(y_ref[...], bw_ref[...])

    pl.when(br == 1)(lambda: accumulate(y1_ref))
    pl.when(br == 2)(lambda: accumulate(y2_ref))
    pl.when((br == 3) & (i < lat_tiles))(lambda: accumulate(y3l_ref))
    pl.when((br == 3) & (i >= lat_tiles))(lambda: accumulate(y3c_ref))


def _merge(h, mods, norm_gain, ys, merge_w, merge_b, branch_w, tiles):
    d = h.shape[1]
    tm, tn = ROW_TILE, 1024
    nc = d // tn
    l, lat_tiles = tiles.layer, tiles.lat_tiles
    depth = merge_b.shape[0]
    y_ret, y_s5, y_gla, y_hy_lat, y_hy_ctx = ys
    yspec = pl.BlockSpec((tm, BRANCH_W), lambda i, c, br: (i, 0))
    return pl.pallas_call(
        functools.partial(_merge_kernel, lat_tiles=lat_tiles),
        grid=(tiles.n_tiles, nc, N_BRANCH),
        in_specs=[pl.BlockSpec((tm, d), lambda i, c, br: (i, 0)),
                  tiles.mod_spec(d),
                  tiles.gain_spec(d),
                  yspec, yspec, yspec,
                  pl.BlockSpec((tm, BRANCH_W), lambda i, c, br: (jnp.minimum(i, lat_tiles - 1), 0)),
                  pl.BlockSpec((tm, BRANCH_W), lambda i, c, br: (jnp.maximum(i - lat_tiles, 0), 0)),
                  pl.BlockSpec((None, d, tn), lambda i, c, br: (l, 0, br * nc + c)),
                  pl.BlockSpec((None, 1, tn), lambda i, c, br: (l, 0, br * nc + c)),
                  pl.BlockSpec((None, None, BRANCH_W, tn), lambda i, c, br: (l, br, 0, c))],
        out_specs=pl.BlockSpec((tm, tn), lambda i, c, br: (i, c)),
        out_shape=jax.ShapeDtypeStruct((tiles.n_tiles * tm, d), F32),
        scratch_shapes=[pltpu.VMEM((tm, d), BF16)],
        compiler_params=_cparams(("parallel", "arbitrary", "arbitrary")),
        name="merge",
    )(h, mods, norm_gain, y_ret, y_s5, y_gla, y_hy_lat, y_hy_ctx, merge_w, merge_b.reshape(depth, 1, -1), branch_w)


def _outproj_kernel(a_ref, h_ref, mod_ref, ng_ref, w_ref, o_ref):
    y = _dot(a_ref[...].astype(BF16), w_ref[...])
    o_ref[...] = h_ref[...] + mod_ref[5:6, :] * _rms(y, ng_ref[3:4, :])


def _outproj(acc, h, mods, norm_gain, w_out, tiles):
    d = h.shape[1]
    tm = ROW_TILE
    l = tiles.layer
    return pl.pallas_call(
        _outproj_kernel,
        grid=(tiles.n_tiles,),
        in_specs=[pl.BlockSpec((tm, d), lambda i: (i, 0)),
                  pl.BlockSpec((tm, d), lambda i: (i, 0)),
                  tiles.mod_spec(d),
                  tiles.gain_spec(d),
                  pl.BlockSpec((None, d, d), lambda i: (l, 0, 0))],
        out_specs=pl.BlockSpec((tm, d), lambda i: (i, 0)),
        out_shape=jax.ShapeDtypeStruct((tiles.n_tiles * tm, d), F32),
        compiler_params=_cparams(("parallel",)),
        name="outproj",
    )(acc, h, mods, norm_gain, w_out)


def _rope_tables(batch, lat_len, ctx_len):
    rows = lat_len // GRID_W
    row = np.repeat(np.arange(rows, dtype=np.float32), GRID_W)
    col = np.tile(np.arange(GRID_W, dtype=np.float32), rows)
    n_freq = RET_DK // 4
    inv = (ROPE_BASE ** (-np.arange(n_freq, dtype=np.float32) / n_freq)).astype(np.float32)
    ang = np.concatenate([row[:, None] * inv, col[:, None] * inv], axis=-1).astype(np.float32)
    cos, sin = np.cos(ang), np.sin(ang)
    cos2 = np.concatenate([cos, cos], axis=-1)
    sin2 = np.concatenate([-sin, sin], axis=-1)
    ctx_rows = batch * ctx_len
    cos_t = np.concatenate([np.tile(cos2, (batch, 1)), np.ones((ctx_rows, RET_DK), np.float32)], axis=0)
    sin_t = np.concatenate([np.tile(sin2, (batch, 1)), np.zeros((ctx_rows, RET_DK), np.float32)], axis=0)
    return jnp.asarray(cos_t, F32), jnp.asarray(sin_t, F32)


def _reorder_w_in(w):
    pad = jnp.zeros(w.shape[:-1] + (P_COLS - w.shape[-1],), w.dtype)
    lr0 = COL_HY
    lr1 = lr0 + 2 * GLA_RANK
    return jnp.concatenate([w[..., :lr0], w[..., lr1:], w[..., lr0:lr1], pad], axis=-1)


def kernel(x, c, ctx, c_ctx, w_ada, b_ada, norm_gain, ffn_w_gate, ffn_w_up, ffn_w_down, w_in, ret_decay_logit, ret_norm_gain, s5_a_re, s5_a_im, s5_log_dt, s5_b_re, s5_b_im, s5_c_re, s5_c_im, s5_d, s5_glu_w, s5_glu_b, gla_gate_w, gla_gate_b, gla_norm_gain, hy_short_w, hy_short_b, hy_w1, hy_b1, hy_w2, hy_b2, hy_w3, hy_freq, hy_bias, branch_w, merge_w, merge_b, w_out):
    batch, lat_len, d = x.shape
    ctx_len = ctx.shape[1]
    depth = w_ada.shape[0]
    lat_rows = batch * lat_len
    tiles_per_batch = lat_len // ROW_TILE
    lat_tiles = batch * tiles_per_batch
    assert lat_len % ROW_TILE == 0 and (batch * ctx_len) % ROW_TILE == 0
    all_tiles = lat_tiles + batch * ctx_len // ROW_TILE
    seq_kw = dict(batch=batch, lat_len=lat_len, ctx_len=ctx_len)

    cond = jnp.zeros((8, d), F32).at[:batch].set(c).at[batch].set(c_ctx)
    mods = _ada(cond, w_ada, b_ada).reshape(depth, 8, N_MOD, d)
    cos_t, sin_t = _rope_tables(batch, lat_len, ctx_len)

    wg, wu, wd = ffn_w_gate.astype(BF16), ffn_w_up.astype(BF16), ffn_w_down.astype(BF16)
    w_in_b = _reorder_w_in(w_in).astype(BF16)
    merge_w_b, branch_w_b, w_out_b, glu_w_b = (t.astype(BF16) for t in (merge_w, branch_w, w_out, s5_glu_w))
    s5_tab = _s5_tables(s5_a_re, s5_a_im, s5_log_dt, s5_b_re, s5_b_im, s5_c_re, s5_c_im)

    h = jnp.concatenate([x.reshape(lat_rows, d), ctx.reshape(batch * ctx_len, d)], axis=0)
    for l in range(depth):
        full = _Tiles(l, lat_tiles, tiles_per_batch, all_tiles)
        tail = full if l < depth - 1 else _Tiles(l, lat_tiles, tiles_per_batch, lat_tiles)
        h = _ffn(h, mods, norm_gain, wg, wu, wd, full, idx=0)
        p = _inproj(h, mods, norm_gain, w_in_b, full)
        y_ret = _retention(p, cos_t, sin_t, ret_decay_logit, ret_norm_gain, layer=l, **seq_kw)
        y_s5 = _s5(p, s5_tab, s5_d, glu_w_b, s5_glu_b, layer=l, **seq_kw)
        y_gla = _gla(p, gla_gate_w, gla_gate_b, gla_norm_gain, layer=l, **seq_kw)
        filt = {n: _hyena_filter(n, hy_w1[l], hy_b1[l], hy_w2[l], hy_b2[l], hy_w3[l], hy_freq[l])
                for n in (lat_len, ctx_len)}
        y_hy_lat, y_hy_ctx = _hyena(p, filt, hy_short_w, hy_short_b, hy_bias, layer=l, **seq_kw)
        acc = _merge(h, mods, norm_gain, (y_ret, y_s5, y_gla, y_hy_lat, y_hy_ctx), merge_w_b, merge_b,
                     branch_w_b, tail)
        h = _outproj(acc, h, mods, norm_gain, w_out_b, tail)
        h = _ffn(h, mods, norm_gain, wg, wu, wd, tail, idx=1)
    return h.reshape(batch, lat_len, d)
```
